```python
import jax, jax.numpy as jnp
from jax import lax
import numpy as np

D_MODEL = 1024
BATCH = 16
SEQ = 2048
DEPTH = 1

CHUNK = 64
MIX_WIDTH = D_MODEL
MLSTM_WIDTH = MIX_WIDTH // 2
MLSTM_HEADS = 4
MLSTM_HEAD_DIM = MLSTM_WIDTH // MLSTM_HEADS
MLSTM_CONV = 4
SB_WIDTH = MIX_WIDTH - MLSTM_WIDTH
SB_HEADS = 8
SB_HEAD_DIM = SB_WIDTH // SB_HEADS
SB_BLOCK = 128
D_FF = 2816
FFN_CONV = 3
EPS = 1e-6
IN_COLS = 4 * MLSTM_WIDTH + 2 * MLSTM_HEADS + 3 * SB_WIDTH

kernel_name = 'hybrid_mlstm_stickbreaking_convffn'


def rms_norm(x, w):
    xf = x.astype(jnp.float32)
    y = xf * lax.rsqrt(jnp.mean(xf * xf, axis=-1, keepdims=True) + EPS)
    return (y * w.astype(jnp.float32)).astype(x.dtype)


def causal_depthwise_conv(x, w, b):
    k_width, ch = w.shape
    y = lax.conv_general_dilated(x, w[:, None, :].astype(x.dtype), window_strides=(1,),
                                 padding=[(k_width - 1, 0)],
                                 dimension_numbers=('NWC', 'WIO', 'NWC'),
                                 feature_group_count=ch)
    return y + b.astype(x.dtype)


def split_heads(t, n_heads):
    b, s, w = t.shape
    return t.reshape(b, s, n_heads, w // n_heads).transpose(0, 2, 1, 3)


def merge_heads(t):
    b, h, s, d = t.shape
    return t.transpose(0, 2, 1, 3).reshape(b, s, h * d)


def mlstm_chunkwise(q, k, v, log_i, log_f):
    b, h, s, d = q.shape
    nc = s // CHUNK
    q = q.astype(jnp.float32)
    k = k.astype(jnp.float32) * (d ** -0.5)
    v = v.astype(jnp.float32)
    log_i = log_i.astype(jnp.float32)
    log_f = log_f.astype(jnp.float32)

    def to_chunks(t):
        return jnp.moveaxis(t.reshape(b, h, nc, CHUNK, *t.shape[3:]), 2, 0)

    causal = jnp.tril(jnp.ones((CHUNK, CHUNK), dtype=bool))

    def step(carry, inp):
        c_state, n_state, m_state = carry
        qc, kc, vc, li, lf = inp
        bcum = jnp.cumsum(lf, axis=-1)
        d_log = jnp.where(causal, bcum[..., :, None] - bcum[..., None, :] + li[..., None, :], -jnp.inf)
        inter = bcum + m_state[..., None]
        m_t = jnp.maximum(jnp.max(d_log, axis=-1), inter)
        w_intra = jnp.exp(d_log - m_t[..., None])
        w_inter = jnp.exp(inter - m_t)
        scores = jnp.einsum('bhtd,bhsd->bhts', qc, kc) * w_intra
        num = (jnp.einsum('bhts,bhse->bhte', scores, vc)
               + w_inter[..., None] * jnp.einsum('bhtd,bhde->bhte', qc, c_state))
        den = jnp.sum(scores, axis=-1) + w_inter * jnp.einsum('bhtd,bhd->bht', qc, n_state)
        h_c = num / jnp.maximum(jnp.abs(den), jnp.exp(-m_t))[..., None]
        b_last = bcum[..., -1]
        g = b_last[..., None] - bcum + li
        a = b_last + m_state
        m_new = jnp.maximum(a, jnp.max(g, axis=-1))
        w_k = jnp.exp(g - m_new[..., None])
        decay = jnp.exp(a - m_new)
        c_new = decay[..., None, None] * c_state + jnp.einsum('bhs,bhsd,bhse->bhde', w_k, kc, vc)
        n_new = decay[..., None] * n_state + jnp.einsum('bhs,bhsd->bhd', w_k, kc)
        return (c_new, n_new, m_new), h_c

    init = (jnp.zeros((b, h, d, d), jnp.float32), jnp.zeros((b, h, d), jnp.float32),
            jnp.zeros((b, h), jnp.float32))
    _, hs = lax.scan(step, init, (to_chunks(q), to_chunks(k), to_chunks(v),
                                  to_chunks(log_i), to_chunks(log_f)))
    return jnp.moveaxis(hs, 0, 2).reshape(b, h, s, d)


def stick_breaking_attention(q, k, v):
    b, h, s, d = q.shape
    scale = d ** -0.5
    q = q.astype(jnp.float32)
    k = k.astype(jnp.float32)
    v = v.astype(jnp.float32)
    outs = []
    for blk in range(s // SB_BLOCK):
        t0 = blk * SB_BLOCK
        t1 = t0 + SB_BLOCK
        z = jnp.einsum('bhtd,bhsd->bhts', q[:, :, t0:t1], k[:, :, :t1]) * scale
        t_idx = t0 + jnp.arange(SB_BLOCK)[:, None]
        s_idx = jnp.arange(t1)[None, :]
        mask = s_idx < t_idx
        log_1mb = jnp.where(mask, jax.nn.log_sigmoid(-z), 0.0)
        rem = lax.cumsum(log_1mb, axis=3, reverse=True) - log_1mb
        attn = jnp.where(mask, jnp.exp(jax.nn.log_sigmoid(z) + rem), 0.0)
        outs.append(jnp.einsum('bhts,bhse->bhte', attn, v[:, :, :t1]))
    return jnp.concatenate(outs, axis=2)


def setup_inputs(seed: int = 0) -> dict:
    key = jax.random.key(seed)
    ks = jax.random.split(key, 20)
    f32 = jnp.float32

    def gain(k_, n):
        return 1.0 + 0.05 * jax.random.normal(k_, (DEPTH, n), f32)

    b_f = (jnp.linspace(3.0, 6.0, MLSTM_HEADS, dtype=f32)[None, :]
           + 0.1 * jax.random.normal(ks[6], (DEPTH, MLSTM_HEADS), f32))
    return {
        'x': jax.random.normal(ks[0], (BATCH, SEQ, D_MODEL), f32),
        'pre_mix_norm': gain(ks[1], D_MODEL),
        'w_in': jax.random.normal(ks[2], (DEPTH, D_MODEL, IN_COLS), f32) * D_MODEL ** -0.5,
        'mlstm_conv_w': jax.random.normal(ks[3], (DEPTH, MLSTM_CONV, 2 * MLSTM_WIDTH), f32) * MLSTM_CONV ** -0.5,
        'mlstm_conv_b': 0.01 * jax.random.normal(ks[4], (DEPTH, 2 * MLSTM_WIDTH), f32),
        'mlstm_b_i': 0.1 * jax.random.normal(ks[5], (DEPTH, MLSTM_HEADS), f32),
        'mlstm_b_f': b_f,
        'mlstm_norm': gain(ks[7], MLSTM_WIDTH),
        'w_out': jax.random.normal(ks[8], (DEPTH, MIX_WIDTH, D_MODEL), f32) * MIX_WIDTH ** -0.5,
        'post_mix_norm': gain(ks[9], D_MODEL),
        'pre_ffn_norm': gain(ks[10], D_MODEL),
        'w_up': jax.random.normal(ks[11], (DEPTH, D_MODEL, 2 * D_FF), f32) * D_MODEL ** -0.5,
        'ffn_conv_w': jax.random.normal(ks[12], (DEPTH, FFN_CONV, D_FF), f32) * FFN_CONV ** -0.5,
        'ffn_conv_b': 0.01 * jax.random.normal(ks[13], (DEPTH, D_FF), f32),
        'w_down': jax.random.normal(ks[14], (DEPTH, D_FF, D_MODEL), f32) * D_FF ** -0.5,
        'post_ffn_norm': gain(ks[15], D_MODEL),
    }


def reference(x, pre_mix_norm, w_in, mlstm_conv_w, mlstm_conv_b, mlstm_b_i, mlstm_b_f,
              mlstm_norm, w_out, post_mix_norm, pre_ffn_norm, w_up, ffn_conv_w, ffn_conv_b,
              w_down, post_ffn_norm):
    bsz, seq, _ = x.shape
    o1 = 2 * MLSTM_WIDTH
    o2 = o1 + MLSTM_WIDTH
    o3 = o2 + MLSTM_WIDTH
    o4 = o3 + MLSTM_HEADS
    o5 = o4 + MLSTM_HEADS
    o6 = o5 + SB_WIDTH
    o7 = o6 + SB_WIDTH
    for l in range(DEPTH):
        h = rms_norm(x, pre_mix_norm[l])
        proj = h @ w_in[l]
        qk_m = jax.nn.silu(causal_depthwise_conv(proj[..., :o1], mlstm_conv_w[l], mlstm_conv_b[l]))
        q_m = split_heads(qk_m[..., :MLSTM_WIDTH], MLSTM_HEADS)
        k_m = split_heads(qk_m[..., MLSTM_WIDTH:], MLSTM_HEADS)
        v_m = split_heads(proj[..., o1:o2], MLSTM_HEADS)
        o_gate = jax.nn.sigmoid(proj[..., o2:o3])
        log_i = (proj[..., o3:o4] + mlstm_b_i[l]).transpose(0, 2, 1)
        log_f = jax.nn.log_sigmoid(proj[..., o4:o5] + mlstm_b_f[l]).transpose(0, 2, 1)
        h_m = mlstm_chunkwise(q_m, k_m, v_m, log_i, log_f)
        h_m = rms_norm(h_m.transpose(0, 2, 1, 3),
                       mlstm_norm[l].reshape(MLSTM_HEADS, MLSTM_HEAD_DIM)).reshape(bsz, seq, MLSTM_WIDTH)
        h_m = (o_gate * h_m).astype(x.dtype)

        q_s = split_heads(proj[..., o5:o6], SB_HEADS)
        k_s = split_heads(proj[..., o6:o7], SB_HEADS)
        v_s = split_heads(proj[..., o7:], SB_HEADS)
        h_s = merge_heads(stick_breaking_attention(q_s, k_s, v_s)).astype(x.dtype)

        mix = jnp.concatenate([h_m, h_s], axis=-1) @ w_out[l]
        x = x + rms_norm(mix, post_mix_norm[l])

        h = rms_norm(x, pre_ffn_norm[l])
        gu = h @ w_up[l]
        gate = causal_depthwise_conv(gu[..., :D_FF], ffn_conv_w[l], ffn_conv_b[l])
        y = (jax.nn.gelu(gate, approximate=True) * gu[..., D_FF:]) @ w_down[l]
        x = x + rms_norm(y, post_ffn_norm[l])
    return x
```

```python
import functools

import jax
import jax.numpy as jnp
from jax import lax
from jax.experimental import pallas as pl
from jax.experimental.pallas import tpu as pltpu

D_MODEL = 1024
MLSTM_WIDTH = 512
MLSTM_HEADS = 4
MLSTM_HEAD_DIM = 128
MLSTM_CONV = 4
SB_WIDTH = 512
SB_HEADS = 8
SB_HEAD_DIM = 64
D_FF = 2816
FFN_CONV = 3
EPS = 1e-6

LANES = 128
SUBLANES = 8
MAIN_COLS = 4 * MLSTM_WIDTH + 3 * SB_WIDTH
GATE_COLS = LANES
PROJ_CHUNK = 512
SEQ_TILE = 512
MLSTM_CHUNK = 128
SB_BLOCK = 128
FF_CHUNK = 256
VMEM_LIMIT = 56 * 1024 * 1024
NEG_BIG = -1e30

F32 = jnp.float32
BF16 = jnp.bfloat16


def _rms(v, w):
    return v * lax.rsqrt(jnp.mean(v * v, axis=-1, keepdims=True) + EPS) * w


def _log_sigmoid(v):
    return jnp.minimum(v, 0.0) - jnp.log1p(jnp.exp(-jnp.abs(v)))


def _dot(a, b):
    return jnp.dot(a, b, preferred_element_type=F32)


def _dot_nt(a, b):
    return lax.dot_general(a, b, (((1,), (1,)), ((), ())), preferred_element_type=F32)


def _dot_tn(a, b):
    return lax.dot_general(a, b, (((0,), (0,)), ((), ())), preferred_element_type=F32)


def _causal_taps(buf_ref, halo_ref, p, taps, width):
    n = p.shape[0]
    buf_ref[0:SUBLANES, :] = halo_ref[...]
    buf_ref[SUBLANES:SUBLANES + n, :] = p
    halo_ref[...] = p[n - SUBLANES:, :]
    y = taps[width - 1] * p
    for j in range(1, width):
        y = y + taps[width - 1 - j] * buf_ref[SUBLANES - j:SUBLANES - j + n, :]
    return y


def _in_proj_kernel(x_ref, g_ref, w_ref, wg_ref, cw_ref, cb_ref, main_ref, gate_ref,
                    buf_ref, halo_ref):
    @pl.when(pl.program_id(1) == 0)
    def _():
        halo_ref[...] = jnp.zeros_like(halo_ref)

    h = _rms(x_ref[0], g_ref[...]).astype(BF16)
    gate_ref[0] = _dot(h, wg_ref[...])
    q_scale = (MLSTM_HEAD_DIM ** -0.5, 1.0)
    for c in range(MAIN_COLS // PROJ_CHUNK):
        lo, hi = c * PROJ_CHUNK, (c + 1) * PROJ_CHUNK
        p = _dot(h, w_ref[:, lo:hi])
        if c < 2:
            taps = [cw_ref[k:k + 1, lo:hi] for k in range(MLSTM_CONV)]
            y = _causal_taps(buf_ref, halo_ref.at[c], p, taps, MLSTM_CONV) + cb_ref[:, lo:hi]
            p = y * jax.nn.sigmoid(y) * q_scale[c]
        elif c == 4:
            p = p * (SB_HEAD_DIM ** -0.5)
        main_ref[0, :, lo:hi] = p.astype(BF16)


def _in_proj(x, g, w_main, w_gate, conv_w, conv_b):
    b, s, d = x.shape
    const = lambda *_: (0, 0)
    return pl.pallas_call(
        _in_proj_kernel,
        grid=(b, s // SEQ_TILE),
        in_specs=[
            pl.BlockSpec((1, SEQ_TILE, d), lambda i, j: (i, j, 0)),
            pl.BlockSpec((1, d), const),
            pl.BlockSpec((d, MAIN_COLS), const),
            pl.BlockSpec((d, GATE_COLS), const),
            pl.BlockSpec((MLSTM_CONV, 2 * MLSTM_WIDTH), const),
            pl.BlockSpec((1, 2 * MLSTM_WIDTH), const),
        ],
        out_specs=[
            pl.BlockSpec((1, SEQ_TILE, MAIN_COLS), lambda i, j: (i, j, 0)),
            pl.BlockSpec((1, SEQ_TILE, GATE_COLS), lambda i, j: (i, j, 0)),
        ],
        out_shape=[
            jax.ShapeDtypeStruct((b, s, MAIN_COLS), BF16),
            jax.ShapeDtypeStruct((b, s, GATE_COLS), F32),
        ],
        scratch_shapes=[
            pltpu.VMEM((SEQ_TILE + SUBLANES, PROJ_CHUNK), F32),
            pltpu.VMEM((2, SUBLANES, PROJ_CHUNK), F32),
        ],
        compiler_params=pltpu.CompilerParams(
            dimension_semantics=("arbitrary", "arbitrary"), vmem_limit_bytes=VMEM_LIMIT),
        name="in_proj",
    )(x, g, w_main, w_gate, conv_w, conv_b)


def _mlstm_kernel(q_ref, k_ref, v_ref, o_ref, gate_ref, gb_ref, nw_ref, out_ref,
                  state_ref, m_ref):
    L = MLSTM_CHUNK
    D = MLSTM_HEAD_DIM
    state_ref[...] = jnp.zeros_like(state_ref)
    m_ref[...] = jnp.zeros_like(m_ref)

    row = lax.broadcasted_iota(jnp.int32, (L, L), 0)
    col = lax.broadcasted_iota(jnp.int32, (L, L), 1)
    causal = col <= row
    tri = causal.astype(F32)
    lane = lax.broadcasted_iota(jnp.int32, (L, LANES), 1)
    ones_col = (lane == 0).astype(BF16)

    def chunk(c, carry):
        t0 = pl.multiple_of(c * L, L)
        gl = gate_ref[0, pl.ds(t0, L), :] + gb_ref[...]
        gsel = jnp.where(lane < MLSTM_HEADS, gl,
                         jnp.where(lane < 2 * MLSTM_HEADS, _log_sigmoid(gl), 0.0))
        cum = jnp.dot(tri, gsel, preferred_element_type=F32, precision=lax.Precision.HIGHEST)
        gsel_t = gsel.T
        cum_t = cum.T
        for h in range(MLSTM_HEADS):
            hs = slice(h * D, (h + 1) * D)
            li_col = gsel[:, h:h + 1]
            li_row = gsel_t[h:h + 1, :]
            b_col = cum[:, MLSTM_HEADS + h:MLSTM_HEADS + h + 1]
            b_row = cum_t[MLSTM_HEADS + h:MLSTM_HEADS + h + 1, :]
            b_last = b_col[L - 1:L, :]
            m_prev = m_ref[h][0:1, 0:1]
            state = state_ref[h]

            d_log = jnp.where(causal, b_col - b_row + li_row, NEG_BIG)
            inter = b_col + m_prev
            m_t = jnp.maximum(jnp.max(d_log, axis=-1, keepdims=True), inter)
            w_intra = jnp.exp(d_log - m_t)
            w_inter = jnp.exp(inter - m_t)

            q = q_ref[0, pl.ds(t0, L), hs]
            k = k_ref[0, pl.ds(t0, L), hs]
            vx = jnp.concatenate([v_ref[0, pl.ds(t0, L), hs], ones_col], axis=1)
            scores = (_dot_nt(q, k) * w_intra).astype(BF16)
            nd = _dot(scores, vx) + w_inter * _dot(q, state.astype(BF16))
            num = nd[:, :D]
            den = nd[:, D:D + 1]
            hc = num / jnp.maximum(jnp.abs(den), jnp.exp(-m_t))

            g = b_last - b_col + li_col
            a = b_last + m_prev
            m_new = jnp.maximum(a, jnp.max(g, axis=0, keepdims=True))
            w_k = jnp.exp(g - m_new)
            decay = jnp.exp(a - m_new)
            wvx = (w_k * vx.astype(F32)).astype(BF16)
            state_ref[h] = decay * state + _dot_tn(k, wvx)
            m_ref[h] = jnp.broadcast_to(m_new, m_ref.shape[1:])

            hn = _rms(hc, nw_ref[:, hs])
            gate = jax.nn.sigmoid(o_ref[0, pl.ds(t0, L), hs].astype(F32))
            out_ref[0, pl.ds(t0, L), hs] = (gate * hn).astype(BF16)
        return carry

    lax.fori_loop(0, q_ref.shape[1] // L, chunk, 0)


def _mlstm(main, gates, gate_bias, norm_w):
    b, s, _ = main.shape
    w = MLSTM_WIDTH
    col = lambda c: pl.BlockSpec((1, s, w), lambda i, c=c: (i, 0, c))
    return pl.pallas_call(
        _mlstm_kernel,
        grid=(b,),
        in_specs=[
            col(0), col(1), col(2), col(3),
            pl.BlockSpec((1, s, GATE_COLS), lambda i: (i, 0, 0)),
            pl.BlockSpec((1, GATE_COLS), lambda i: (0, 0)),
            pl.BlockSpec((1, w), lambda i: (0, 0)),
        ],
        out_specs=pl.BlockSpec((1, s, w), lambda i: (i, 0, 0)),
        out_shape=jax.ShapeDtypeStruct((b, s, w), BF16),
        scratch_shapes=[
            pltpu.VMEM((MLSTM_HEADS, MLSTM_HEAD_DIM, 2 * MLSTM_HEAD_DIM), F32),
            pltpu.VMEM((MLSTM_HEADS, SUBLANES, LANES), F32),
        ],
        compiler_params=pltpu.CompilerParams(
            dimension_semantics=("arbitrary",), vmem_limit_bytes=VMEM_LIMIT),
        name="mlstm",
    )(main, main, main, main, gates, gate_bias, norm_w)


def _sb_kernel(q_ref, k_ref, v_ref, out_ref, acc_ref, carry_ref):
    T = SB_BLOCK
    lane = lax.broadcasted_iota(jnp.int32, (T, LANES), 1)
    head0 = lane < SB_HEAD_DIM
    row = lax.broadcasted_iota(jnp.int32, (T, T), 0)
    col = lax.broadcasted_iota(jnp.int32, (T, T), 1)
    strict = col < row
    later = jnp.concatenate([(row > col).astype(BF16), jnp.ones((T, T), BF16)], axis=1)

    def softplus(z):
        return jnp.maximum(z, 0.0) + jnp.log(1.0 + jnp.exp(-jnp.abs(z)))

    def q_block(i, carry):
        t0 = pl.multiple_of(i * T, T)
        q = q_ref[0, pl.ds(t0, T), :]
        qh = (jnp.where(head0, q, jnp.zeros_like(q)), jnp.where(head0, jnp.zeros_like(q), q))

        k = k_ref[0, pl.ds(t0, T), :]
        v = v_ref[0, pl.ds(t0, T), :]
        for h in range(2):
            z = _dot_nt(qh[h], k)
            sp = jnp.where(strict, softplus(z), 0.0)
            rt = _dot(sp.astype(BF16), later)
            a = jnp.where(strict, jnp.exp(z - sp - rt[:, :T]), 0.0)
            acc_ref[h] = _dot(a.astype(BF16), v)
            carry_ref[h] = rt[:, T:]

        def k_block(n, carry2):
            s0 = pl.multiple_of((i - n) * T, T)
            k = k_ref[0, pl.ds(s0, T), :]
            v = v_ref[0, pl.ds(s0, T), :]
            for h in range(2):
                z = _dot_nt(qh[h], k)
                sp = softplus(z)
                rt = _dot(sp.astype(BF16), later)
                cp = carry_ref[h]
                a = jnp.exp(z - sp - rt[:, :T] - cp)
                acc_ref[h] += _dot(a.astype(BF16), v)
                carry_ref[h] = cp + rt[:, T:]
            return carry2

        lax.fori_loop(1, i + 1, k_block, 0)
        out_ref[0, pl.ds(t0, T), :] = jnp.where(head0, acc_ref[0], acc_ref[1]).astype(BF16)
        return carry

    lax.fori_loop(0, q_ref.shape[1] // T, q_block, 0)


def _sb_attn(main):
    b, s, _ = main.shape
    pairs = SB_WIDTH // LANES
    base = 4 * MLSTM_WIDTH // LANES
    col = lambda off: pl.BlockSpec((1, s, LANES), lambda i, p, off=off: (i, 0, base + off + p))
    return pl.pallas_call(
        _sb_kernel,
        grid=(b, pairs),
        in_specs=[col(0), col(pairs), col(2 * pairs)],
        out_specs=pl.BlockSpec((1, s, LANES), lambda i, p: (i, 0, p)),
        out_shape=jax.ShapeDtypeStruct((b, s, SB_WIDTH), BF16),
        scratch_shapes=[
            pltpu.VMEM((2, SB_BLOCK, LANES), F32),
            pltpu.VMEM((2, SB_BLOCK, SB_BLOCK), F32),
        ],
        compiler_params=pltpu.CompilerParams(
            dimension_semantics=("arbitrary", "arbitrary"), vmem_limit_bytes=VMEM_LIMIT),
        name="sb_attn",
    )(main, main, main)


def _out_ffn_kernel(x_ref, hm_ref, hs_ref, wo_ref, g1_ref, g2_ref, wu_ref, cw_ref, cb_ref,
                    wd_ref, g3_ref, out_ref, x1_ref, h_ref, act_ref, buf_ref, halo_ref):
    @pl.when(pl.program_id(1) == 0)
    def _():
        halo_ref[...] = jnp.zeros_like(halo_ref)

    mix = _dot(hm_ref[0], wo_ref[:MLSTM_WIDTH, :]) + _dot(hs_ref[0], wo_ref[MLSTM_WIDTH:, :])
    x1 = x_ref[0] + _rms(mix, g1_ref[...])
    x1_ref[...] = x1
    h_ref[...] = _rms(x1, g2_ref[...]).astype(BF16)

    for c in range(D_FF // FF_CHUNK):
        lo, hi = c * FF_CHUNK, (c + 1) * FF_CHUNK
        h = h_ref[...]
        gate = _dot(h, wu_ref[:, lo:hi])
        up = _dot(h, wu_ref[:, D_FF + lo:D_FF + hi])
        taps = [cw_ref[k:k + 1, lo:hi] for k in range(FFN_CONV)]
        y = _causal_taps(buf_ref, halo_ref.at[c], gate, taps, FFN_CONV) + cb_ref[:, lo:hi]
        act_ref[:, lo:hi] = (jax.nn.gelu(y, approximate=True) * up).astype(BF16)

    y = _dot(act_ref[...], wd_ref[...])
    out_ref[0] = x1_ref[...] + _rms(y, g3_ref[...])


def _out_ffn(x, h_m, h_s, w_out, g1, g2, w_up, conv_w, conv_b, w_down, g3):
    b, s, d = x.shape
    const = lambda *_: (0, 0)
    tile = lambda w: pl.BlockSpec((1, SEQ_TILE, w), lambda i, j: (i, j, 0))
    resident = lambda shape: pl.BlockSpec(shape, const, pipeline_mode=pl.Buffered(1))
    return pl.pallas_call(
        _out_ffn_kernel,
        grid=(b, s // SEQ_TILE),
        in_specs=[
            tile(d), tile(MLSTM_WIDTH), tile(SB_WIDTH),
            resident((d, d)),
            pl.BlockSpec((1, d), const), pl.BlockSpec((1, d), const),
            resident((d, 2 * D_FF)),
            pl.BlockSpec((FFN_CONV, D_FF), const), pl.BlockSpec((1, D_FF), const),
            resident((D_FF, d)),
            pl.BlockSpec((1, d), const),
        ],
        out_specs=tile(d),
        out_shape=jax.ShapeDtypeStruct((b, s, d), F32),
        scratch_shapes=[
            pltpu.VMEM((SEQ_TILE, d), F32),
            pltpu.VMEM((SEQ_TILE, d), BF16),
            pltpu.VMEM((SEQ_TILE, D_FF), BF16),
            pltpu.VMEM((SEQ_TILE + SUBLANES, FF_CHUNK), F32),
            pltpu.VMEM((D_FF // FF_CHUNK, SUBLANES, FF_CHUNK), F32),
        ],
        compiler_params=pltpu.CompilerParams(
            dimension_semantics=("arbitrary", "arbitrary"), vmem_limit_bytes=VMEM_LIMIT),
        name="out_ffn",
    )(x, h_m, h_s, w_out, g1, g2, w_up, conv_w, conv_b, w_down, g3)


@jax.jit
def kernel(x, pre_mix_norm, w_in, mlstm_conv_w, mlstm_conv_b, mlstm_b_i, mlstm_b_f, mlstm_norm,
           w_out, post_mix_norm, pre_ffn_norm, w_up, ffn_conv_w, ffn_conv_b, w_down,
           post_ffn_norm):
    depth = w_in.shape[0]
    o_gate = 4 * MLSTM_WIDTH
    o_sb = o_gate + 2 * MLSTM_HEADS
    for l in range(depth):
        w = w_in[l]
        w_main = jnp.concatenate([w[:, :o_gate], w[:, o_sb:]], axis=1).astype(BF16)
        w_gate = jnp.pad(w[:, o_gate:o_sb], ((0, 0), (0, GATE_COLS - 2 * MLSTM_HEADS))).astype(BF16)
        gate_bias = jnp.pad(jnp.concatenate([mlstm_b_i[l], mlstm_b_f[l]]),
                            (0, GATE_COLS - 2 * MLSTM_HEADS))[None, :]
        main, gates = _in_proj(x, pre_mix_norm[l][None, :], w_main, w_gate,
                               mlstm_conv_w[l], mlstm_conv_b[l][None, :])
        h_m = _mlstm(main, gates, gate_bias, mlstm_norm[l][None, :])
        h_s = _sb_attn(main)
        x = _out_ffn(x, h_m, h_s, w_out[l].astype(BF16), post_mix_norm[l][None, :],
                     pre_ffn_norm[l][None, :], w_up[l].astype(BF16), ffn_conv_w[l],
                     ffn_conv_b[l][None, :], w_down[l].astype(BF16), post_ffn_norm[l][None, :])
    return x
```

```python
import functools

import jax
import jax.numpy as jnp
from jax import lax
from jax.experimental import pallas as pl
from jax.experimental.pallas import tpu as pltpu

D_MODEL = 1024
MLSTM_WIDTH = 512
MLSTM_HEADS = 4
MLSTM_HEAD_DIM = 128
MLSTM_CONV = 4
SB_WIDTH = 512
SB_HEADS = 8
SB_HEAD_DIM = 64
D_FF = 2816
FFN_CONV = 3
EPS = 1e-6

LANES = 128
SUBLANES = 8
MAIN_COLS = 4 * MLSTM_WIDTH + 3 * SB_WIDTH
GATE_COLS = LANES
PROJ_CHUNK = 512
SEQ_TILE = 512
MLSTM_CHUNK = 256
SB_BLOCK = 256
LOG2E = 1.4426950408889634
SB_SKIP_LOG2 = 151.0
FF_CHUNK = 256
VMEM_LIMIT = 56 * 1024 * 1024
NEG_BIG = -1e30

F32 = jnp.float32
BF16 = jnp.bfloat16


def _rms(v, w):
    return v * lax.rsqrt(jnp.mean(v * v, axis=-1, keepdims=True) + EPS) * w


def _log_sigmoid(v):
    return jnp.minimum(v, 0.0) - jnp.log1p(jnp.exp(-jnp.abs(v)))


def _dot(a, b):
    return jnp.dot(a, b, preferred_element_type=F32)


def _dot_nt(a, b):
    return lax.dot_general(a, b, (((1,), (1,)), ((), ())), preferred_element_type=F32)


def _dot_tn(a, b):
    return lax.dot_general(a, b, (((0,), (0,)), ((), ())), preferred_element_type=F32)


def _causal_taps(buf_ref, halo_ref, p, taps, width):
    n = p.shape[0]
    buf_ref[0:SUBLANES, :] = halo_ref[...]
    buf_ref[SUBLANES:SUBLANES + n, :] = p
    halo_ref[...] = p[n - SUBLANES:, :]
    y = taps[width - 1] * p
    for j in range(1, width):
        y = y + taps[width - 1 - j] * buf_ref[SUBLANES - j:SUBLANES - j + n, :]
    return y


def _in_proj_kernel(x_ref, g_ref, w_ref, wg_ref, cw_ref, cb_ref, main_ref, gate_ref,
                    buf_ref, halo_ref):
    @pl.when(pl.program_id(1) == 0)
    def _():
        halo_ref[...] = jnp.zeros_like(halo_ref)

    h = _rms(x_ref[0], g_ref[...]).astype(BF16)
    gate_ref[0] = _dot(h, wg_ref[...])
    q_scale = (MLSTM_HEAD_DIM ** -0.5, 1.0)
    for c in range(MAIN_COLS // PROJ_CHUNK):
        lo, hi = c * PROJ_CHUNK, (c + 1) * PROJ_CHUNK
        p = _dot(h, w_ref[:, lo:hi])
        if c < 2:
            taps = [cw_ref[k:k + 1, lo:hi] for k in range(MLSTM_CONV)]
            y = _causal_taps(buf_ref, halo_ref.at[c], p, taps, MLSTM_CONV) + cb_ref[:, lo:hi]
            p = y * jax.nn.sigmoid(y) * q_scale[c]
        elif c == 4:
            p = p * (SB_HEAD_DIM ** -0.5 * LOG2E)
        main_ref[0, :, lo:hi] = p.astype(BF16)


def _in_proj(x, g, w_main, w_gate, conv_w, conv_b):
    b, s, d = x.shape
    const = lambda *_: (0, 0)
    return pl.pallas_call(
        _in_proj_kernel,
        grid=(b, s // SEQ_TILE),
        in_specs=[
            pl.BlockSpec((1, SEQ_TILE, d), lambda i, j: (i, j, 0)),
            pl.BlockSpec((1, d), const),
            pl.BlockSpec((d, MAIN_COLS), const),
            pl.BlockSpec((d, GATE_COLS), const),
            pl.BlockSpec((MLSTM_CONV, 2 * MLSTM_WIDTH), const),
            pl.BlockSpec((1, 2 * MLSTM_WIDTH), const),
        ],
        out_specs=[
            pl.BlockSpec((1, SEQ_TILE, MAIN_COLS), lambda i, j: (i, j, 0)),
            pl.BlockSpec((1, SEQ_TILE, GATE_COLS), lambda i, j: (i, j, 0)),
        ],
        out_shape=[
            jax.ShapeDtypeStruct((b, s, MAIN_COLS), BF16),
            jax.ShapeDtypeStruct((b, s, GATE_COLS), F32),
        ],
        scratch_shapes=[
            pltpu.VMEM((SEQ_TILE + SUBLANES, PROJ_CHUNK), F32),
            pltpu.VMEM((2, SUBLANES, PROJ_CHUNK), F32),
        ],
        compiler_params=pltpu.CompilerParams(
            dimension_semantics=("arbitrary", "arbitrary"), vmem_limit_bytes=VMEM_LIMIT),
        name="in_proj",
    )(x, g, w_main, w_gate, conv_w, conv_b)


def _mlstm_kernel(q_ref, k_ref, v_ref, o_ref, gate_ref, gb_ref, nw_ref, out_ref,
                  state_ref, m_ref):
    L = MLSTM_CHUNK
    D = MLSTM_HEAD_DIM
    state_ref[...] = jnp.zeros_like(state_ref)
    m_ref[...] = jnp.zeros_like(m_ref)

    row = lax.broadcasted_iota(jnp.int32, (L, L), 0)
    col = lax.broadcasted_iota(jnp.int32, (L, L), 1)
    causal = col <= row
    tri = causal.astype(F32)
    lane = lax.broadcasted_iota(jnp.int32, (L, LANES), 1)
    ones_col = (lane == 0).astype(BF16)

    def chunk(c, carry):
        t0 = pl.multiple_of(c * L, L)
        gl = gate_ref[0, pl.ds(t0, L), :] + gb_ref[...]
        gsel = jnp.where(lane < MLSTM_HEADS, gl,
                         jnp.where(lane < 2 * MLSTM_HEADS, _log_sigmoid(gl), 0.0))
        cum = jnp.dot(tri, gsel, preferred_element_type=F32, precision=lax.Precision.HIGHEST)
        gsel_t = gsel.T
        cum_t = cum.T
        for h in range(MLSTM_HEADS):
            hs = slice(h * D, (h + 1) * D)
            li_col = gsel[:, h:h + 1]
            li_row = gsel_t[h:h + 1, :]
            b_col = cum[:, MLSTM_HEADS + h:MLSTM_HEADS + h + 1]
            b_row = cum_t[MLSTM_HEADS + h:MLSTM_HEADS + h + 1, :]
            b_last = b_col[L - 1:L, :]
            m_prev = m_ref[h][0:1, 0:1]
            state = state_ref[h]

            d_log = jnp.where(causal, b_col - b_row + li_row, NEG_BIG)
            inter = b_col + m_prev
            m_t = jnp.maximum(jnp.max(d_log, axis=-1, keepdims=True), inter)
            w_intra = jnp.exp(d_log - m_t)
            w_inter = jnp.exp(inter - m_t)

            q = q_ref[0, pl.ds(t0, L), hs]
            k = k_ref[0, pl.ds(t0, L), hs]
            vx = jnp.concatenate([v_ref[0, pl.ds(t0, L), hs], ones_col], axis=1)
            scores = (_dot_nt(q, k) * w_intra).astype(BF16)
            nd = _dot(scores, vx) + w_inter * _dot(q, state.astype(BF16))
            num = nd[:, :D]
            den = nd[:, D:D + 1]
            hc = num / jnp.maximum(jnp.abs(den), jnp.exp(-m_t))

            g = b_last - b_col + li_col
            a = b_last + m_prev
            m_new = jnp.maximum(a, jnp.max(g, axis=0, keepdims=True))
            w_k = jnp.exp(g - m_new)
            decay = jnp.exp(a - m_new)
            wvx = (w_k * vx.astype(F32)).astype(BF16)
            state_ref[h] = decay * state + _dot_tn(k, wvx)
            m_ref[h] = jnp.broadcast_to(m_new, m_ref.shape[1:])

            hn = _rms(hc, nw_ref[:, hs])
            gate = jax.nn.sigmoid(o_ref[0, pl.ds(t0, L), hs].astype(F32))
            out_ref[0, pl.ds(t0, L), hs] = (gate * hn).astype(BF16)
        return carry

    lax.fori_loop(0, q_ref.shape[1] // L, chunk, 0)


def _mlstm(main, gates, gate_bias, norm_w):
    b, s, _ = main.shape
    w = MLSTM_WIDTH
    col = lambda c: pl.BlockSpec((1, s, w), lambda i, c=c: (i, 0, c))
    return pl.pallas_call(
        _mlstm_kernel,
        grid=(b,),
        in_specs=[
            col(0), col(1), col(2), col(3),
            pl.BlockSpec((1, s, GATE_COLS), lambda i: (i, 0, 0)),
            pl.BlockSpec((1, GATE_COLS), lambda i: (0, 0)),
            pl.BlockSpec((1, w), lambda i: (0, 0)),
        ],
        out_specs=pl.BlockSpec((1, s, w), lambda i: (i, 0, 0)),
        out_shape=jax.ShapeDtypeStruct((b, s, w), BF16),
        scratch_shapes=[
            pltpu.VMEM((MLSTM_HEADS, MLSTM_HEAD_DIM, 2 * MLSTM_HEAD_DIM), F32),
            pltpu.VMEM((MLSTM_HEADS, SUBLANES, LANES), F32),
        ],
        compiler_params=pltpu.CompilerParams(
            dimension_semantics=("arbitrary",), vmem_limit_bytes=VMEM_LIMIT),
        name="mlstm",
    )(main, main, main, main, gates, gate_bias, norm_w)


def _sb_kernel(q_ref, k_ref, v_ref, out_ref, acc_ref, carry_ref):
    T = SB_BLOCK
    pairs = SB_WIDTH // LANES
    lane = lax.broadcasted_iota(jnp.int32, (T, LANES), 1)
    head0 = lane < SB_HEAD_DIM
    row2 = lax.broadcasted_iota(jnp.int32, (2 * T, T), 0)
    col2 = lax.broadcasted_iota(jnp.int32, (2 * T, T), 1)
    strict = col2 < jnp.bitwise_and(row2, T - 1)
    row = lax.broadcasted_iota(jnp.int32, (T, T), 0)
    col = lax.broadcasted_iota(jnp.int32, (T, T), 1)
    later = (row > col).astype(BF16)

    def lanes(p):
        return slice(p * LANES, (p + 1) * LANES)

    def stacked_q(t0, p):
        q = q_ref[0, pl.ds(t0, T), lanes(p)]
        zero = jnp.zeros_like(q)
        return jnp.concatenate([jnp.where(head0, q, zero), jnp.where(head0, zero, q)], axis=0)

    def tile(qq, s0, p, shift, mask):
        k = k_ref[0, pl.ds(s0, T), lanes(p)]
        v = v_ref[0, pl.ds(s0, T), lanes(p)]
        z = _dot_nt(qq, k)
        t = jnp.minimum(z, 0.0) - jnp.log2(1.0 + jnp.exp2(-jnp.abs(z)))
        sp = z - t
        if mask is not None:
            sp = jnp.where(mask, sp, 0.0)
        spb = sp.astype(BF16)
        r = _dot(spb, later)
        arg = t - r
        if shift is not None:
            arg = arg - shift
        a = jnp.exp2(arg)
        if mask is not None:
            a = jnp.where(mask, a, 0.0)
        o = _dot(a.astype(BF16), v)
        total = r[:, :1] + spb[:, :1].astype(F32)
        return o, total

    def merge_heads(o):
        return jnp.where(head0, o[:T], o[T:]).astype(BF16)

    for p in range(pairs):
        o, _ = tile(stacked_q(0, p), 0, p, None, strict)
        out_ref[0, 0:T, lanes(p)] = merge_heads(o)

    def q_block(i, carry):
        t0 = pl.multiple_of(i * T, T)
        t1 = pl.multiple_of(t0 - T, T)
        low = None
        for p in range(pairs):
            qq = stacked_q(t0, p)
            o0, tot0 = tile(qq, t0, p, None, strict)
            o1, tot1 = tile(qq, t1, p, tot0, None)
            acc_ref[p] = o0 + o1
            cp = tot0 + tot1
            carry_ref[p] = cp
            m = jnp.min(cp)
            low = m if low is None else jnp.minimum(low, m)

        def more(state):
            d, low = state
            return jnp.logical_and(d <= i, low < SB_SKIP_LOG2)

        def k_block(state):
            d, _ = state
            s0 = pl.multiple_of((i - d) * T, T)
            low = None
            for p in range(pairs):
                cp = carry_ref[p]
                o, tot = tile(stacked_q(t0, p), s0, p, cp, None)
                acc_ref[p] += o
                cp = cp + tot
                carry_ref[p] = cp
                m = jnp.min(cp)
                low = m if low is None else jnp.minimum(low, m)
            return d + 1, low

        lax.while_loop(more, k_block, (jnp.int32(2), low))
        for p in range(pairs):
            out_ref[0, pl.ds(t0, T), lanes(p)] = merge_heads(acc_ref[p])
        return carry

    lax.fori_loop(1, q_ref.shape[1] // T, q_block, 0)


def _sb_attn(main):
    b, s, _ = main.shape
    base = 4 * MLSTM_WIDTH // SB_WIDTH
    col = lambda off: pl.BlockSpec((1, s, SB_WIDTH), lambda i, off=off: (i, 0, base + off))
    return pl.pallas_call(
        _sb_kernel,
        grid=(b,),
        in_specs=[col(0), col(1), col(2)],
        out_specs=pl.BlockSpec((1, s, SB_WIDTH), lambda i: (i, 0, 0)),
        out_shape=jax.ShapeDtypeStruct((b, s, SB_WIDTH), BF16),
        scratch_shapes=[
            pltpu.VMEM((SB_WIDTH // LANES, 2 * SB_BLOCK, LANES), F32),
            pltpu.VMEM((SB_WIDTH // LANES, 2 * SB_BLOCK, 1), F32),
        ],
        compiler_params=pltpu.CompilerParams(
            dimension_semantics=("arbitrary",), vmem_limit_bytes=VMEM_LIMIT),
        name="sb_attn",
    )(main, main, main)


def _out_ffn_kernel(x_ref, hm_ref, hs_ref, wo_ref, g1_ref, g2_ref, wu_ref, cw_ref, cb_ref,
                    wd_ref, g3_ref, out_ref, x1_ref, h_ref, act_ref, buf_ref, halo_ref):
    @pl.when(pl.program_id(1) == 0)
    def _():
        halo_ref[...] = jnp.zeros_like(halo_ref)

    mix = _dot(hm_ref[0], wo_ref[:MLSTM_WIDTH, :]) + _dot(hs_ref[0], wo_ref[MLSTM_WIDTH:, :])
    x1 = x_ref[0] + _rms(mix, g1_ref[...])
    x1_ref[...] = x1
    h_ref[...] = _rms(x1, g2_ref[...]).astype(BF16)

    for c in range(D_FF // FF_CHUNK):
        lo, hi = c * FF_CHUNK, (c + 1) * FF_CHUNK
        h = h_ref[...]
        gate = _dot(h, wu_ref[:, lo:hi])
        up = _dot(h, wu_ref[:, D_FF + lo:D_FF + hi])
        taps = [cw_ref[k:k + 1, lo:hi] for k in range(FFN_CONV)]
        y = _causal_taps(buf_ref, halo_ref.at[c], gate, taps, FFN_CONV) + cb_ref[:, lo:hi]
        act_ref[:, lo:hi] = (jax.nn.gelu(y, approximate=True) * up).astype(BF16)

    y = _dot(act_ref[...], wd_ref[...])
    out_ref[0] = x1_ref[...] + _rms(y, g3_ref[...])


def _out_ffn(x, h_m, h_s, w_out, g1, g2, w_up, conv_w, conv_b, w_down, g3):
    b, s, d = x.shape
    const = lambda *_: (0, 0)
    tile = lambda w: pl.BlockSpec((1, SEQ_TILE, w), lambda i, j: (i, j, 0))
    resident = lambda shape: pl.BlockSpec(shape, const, pipeline_mode=pl.Buffered(1))
    return pl.pallas_call(
        _out_ffn_kernel,
        grid=(b, s // SEQ_TILE),
        in_specs=[
            tile(d), tile(MLSTM_WIDTH), tile(SB_WIDTH),
            resident((d, d)),
            pl.BlockSpec((1, d), const), pl.BlockSpec((1, d), const),
            resident((d, 2 * D_FF)),
            pl.BlockSpec((FFN_CONV, D_FF), const), pl.BlockSpec((1, D_FF), const),
            resident((D_FF, d)),
            pl.BlockSpec((1, d), const),
        ],
        out_specs=tile(d),
        out_shape=jax.ShapeDtypeStruct((b, s, d), F32),
        scratch_shapes=[
            pltpu.VMEM((SEQ_TILE, d), F32),
            pltpu.VMEM((SEQ_TILE, d), BF16),
            pltpu.VMEM((SEQ_TILE, D_FF), BF16),
            pltpu.VMEM((SEQ_TILE + SUBLANES, FF_CHUNK), F32),
            pltpu.VMEM((D_FF // FF_CHUNK, SUBLANES, FF_CHUNK), F32),
        ],
        compiler_params=pltpu.CompilerParams(
            dimension_semantics=("arbitrary", "arbitrary"), vmem_limit_bytes=VMEM_LIMIT),
        name="out_ffn",
    )(x, h_m, h_s, w_out, g1, g2, w_up, conv_w, conv_b, w_down, g3)


@jax.jit
def kernel(x, pre_mix_norm, w_in, mlstm_conv_w, mlstm_conv_b, mlstm_b_i, mlstm_b_f, mlstm_norm,
           w_out, post_mix_norm, pre_ffn_norm, w_up, ffn_conv_w, ffn_conv_b, w_down,
           post_ffn_norm):
    depth = w_in.shape[0]
    o_gate = 4 * MLSTM_WIDTH
    o_sb = o_gate + 2 * MLSTM_HEADS
    for l in range(depth):
        w = w_in[l]
        w_main = jnp.concatenate([w[:, :o_gate], w[:, o_sb:]], axis=1).astype(BF16)
        w_gate = jnp.pad(w[:, o_gate:o_sb], ((0, 0), (0, GATE_COLS - 2 * MLSTM_HEADS))).astype(BF16)
        gate_bias = jnp.pad(jnp.concatenate([mlstm_b_i[l], mlstm_b_f[l]]),
                            (0, GATE_COLS - 2 * MLSTM_HEADS))[None, :]
        main, gates = _in_proj(x, pre_mix_norm[l][None, :], w_main, w_gate,
                               mlstm_conv_w[l], mlstm_conv_b[l][None, :])
        h_m = _mlstm(main, gates, gate_bias, mlstm_norm[l][None, :])
        h_s = _sb_attn(main)
        x = _out_ffn(x, h_m, h_s, w_out[l].astype(BF16), post_mix_norm[l][None, :],
                     pre_ffn_norm[l][None, :], w_up[l].astype(BF16), ffn_conv_w[l],
                     ffn_conv_b[l][None, :], w_down[l].astype(BF16), post_ffn_norm[l][None, :])
    return x
```

```python
import functools

import jax
import jax.numpy as jnp
from jax import lax
from jax.experimental import pallas as pl
from jax.experimental.pallas import tpu as pltpu

D_MODEL = 1024
MLSTM_WIDTH = 512
MLSTM_HEADS = 4
MLSTM_HEAD_DIM = 128
MLSTM_CONV = 4
SB_WIDTH = 512
SB_HEADS = 8
SB_HEAD_DIM = 64
D_FF = 2816
FFN_CONV = 3
EPS = 1e-6

LANES = 128
SUBLANES = 8
MAIN_COLS = 4 * MLSTM_WIDTH + 3 * SB_WIDTH
GATE_COLS = LANES
PROJ_CHUNK = 256
SEQ_TILE = 512
MLSTM_CHUNK = 256
SB_BLOCK = 256
LOG2E = 1.4426950408889634
SB_SKIP_LOG2 = 151.0
FF_CHUNK = 256
VMEM_LIMIT = 56 * 1024 * 1024
NEG_BIG = -1e30

F32 = jnp.float32
BF16 = jnp.bfloat16


def _rms(v, w):
    return v * lax.rsqrt(jnp.mean(v * v, axis=-1, keepdims=True) + EPS) * w


def _log_sigmoid(v):
    return jnp.minimum(v, 0.0) - jnp.log1p(jnp.exp(-jnp.abs(v)))


def _dot(a, b):
    return jnp.dot(a, b, preferred_element_type=F32)


def _dot_nt(a, b):
    return lax.dot_general(a, b, (((1,), (1,)), ((), ())), preferred_element_type=F32)


def _dot_tn(a, b):
    return lax.dot_general(a, b, (((0,), (0,)), ((), ())), preferred_element_type=F32)


def _causal_taps(buf_ref, halo_ref, p, taps, width):
    n = p.shape[0]
    buf_ref[0:SUBLANES, :] = halo_ref[...]
    buf_ref[SUBLANES:SUBLANES + n, :] = p
    halo_ref[...] = p[n - SUBLANES:, :]
    y = taps[width - 1] * p
    for j in range(1, width):
        y = y + taps[width - 1 - j] * buf_ref[SUBLANES - j:SUBLANES - j + n, :]
    return y


def _in_proj_kernel(x_ref, g_ref, w_ref, wg_ref, cw_ref, cb_ref, main_ref, gate_ref,
                    h_ref, buf_ref, halo_ref):
    @pl.when(pl.program_id(1) == 0)
    def _():
        halo_ref[...] = jnp.zeros_like(halo_ref)

    h_slot = h_ref.at[pl.program_id(1) % 2]
    h_slot[...] = _rms(x_ref[0], g_ref[...]).astype(BF16)
    gate_ref[0] = _dot(h_slot[...], wg_ref[...])
    sb_q = 4 * MLSTM_WIDTH
    n_conv = 2 * MLSTM_WIDTH // PROJ_CHUNK
    n_all = MAIN_COLS // PROJ_CHUNK
    plain = list(range(n_conv, n_all))
    per = -(-len(plain) // n_conv)
    order = []
    for c in range(n_conv):
        order += [c] + plain[c * per:(c + 1) * per]
    for c in order:
        lo, hi = c * PROJ_CHUNK, (c + 1) * PROJ_CHUNK
        p = _dot(h_slot[...], w_ref[:, lo:hi])
        if hi <= 2 * MLSTM_WIDTH:
            taps = [cw_ref[k:k + 1, lo:hi] for k in range(MLSTM_CONV)]
            y = _causal_taps(buf_ref, halo_ref.at[c], p, taps, MLSTM_CONV) + cb_ref[:, lo:hi]
            p = y * jax.nn.sigmoid(y)
            if hi <= MLSTM_WIDTH:
                p = p * MLSTM_HEAD_DIM ** -0.5
        elif sb_q <= lo and hi <= sb_q + SB_WIDTH:
            p = p * (SB_HEAD_DIM ** -0.5 * LOG2E)
        main_ref[0, :, lo:hi] = p.astype(BF16)


def _in_proj(x, g, w_main, w_gate, conv_w, conv_b):
    b, s, d = x.shape
    const = lambda *_: (0, 0)
    return pl.pallas_call(
        _in_proj_kernel,
        grid=(b, s // SEQ_TILE),
        in_specs=[
            pl.BlockSpec((1, SEQ_TILE, d), lambda i, j: (i, j, 0)),
            pl.BlockSpec((1, d), const),
            pl.BlockSpec((d, MAIN_COLS), const),
            pl.BlockSpec((d, GATE_COLS), const),
            pl.BlockSpec((MLSTM_CONV, 2 * MLSTM_WIDTH), const),
            pl.BlockSpec((1, 2 * MLSTM_WIDTH), const),
        ],
        out_specs=[
            pl.BlockSpec((1, SEQ_TILE, MAIN_COLS), lambda i, j: (i, j, 0)),
            pl.BlockSpec((1, SEQ_TILE, GATE_COLS), lambda i, j: (i, j, 0)),
        ],
        out_shape=[
            jax.ShapeDtypeStruct((b, s, MAIN_COLS), BF16),
            jax.ShapeDtypeStruct((b, s, GATE_COLS), F32),
        ],
        scratch_shapes=[
            pltpu.VMEM((2, SEQ_TILE, d), BF16),
            pltpu.VMEM((SEQ_TILE + SUBLANES, PROJ_CHUNK), F32),
            pltpu.VMEM((2 * MLSTM_WIDTH // PROJ_CHUNK, SUBLANES, PROJ_CHUNK), F32),
        ],
        compiler_params=pltpu.CompilerParams(
            dimension_semantics=("arbitrary", "arbitrary"), vmem_limit_bytes=VMEM_LIMIT),
        name="in_proj",
    )(x, g, w_main, w_gate, conv_w, conv_b)


def _mlstm_kernel(q_ref, k_ref, v_ref, o_ref, gate_ref, gb_ref, nw_ref, out_ref,
                  state_ref, a_ref, dec_ref, col_ref):
    L = MLSTM_CHUNK
    D = MLSTM_HEAD_DIM
    H = MLSTM_HEADS
    R = SUBLANES
    n_chunks = q_ref.shape[1] // L
    assert L >= 2 * D and 4 * R <= LANES

    gt = (gate_ref[0] + gb_ref[...]).T[:R, :]
    lf = _log_sigmoid(pltpu.roll(gt, H, axis=0))
    by_chunk = lambda x: jnp.concatenate([x[:, c * L:(c + 1) * L] for c in range(n_chunks)], axis=0)
    u = lax.broadcasted_iota(jnp.int32, (L, L), 0)
    t = lax.broadcasted_iota(jnp.int32, (L, L), 1)
    b = jnp.dot(by_chunk(lf), (u <= t).astype(F32), preferred_element_type=F32,
                precision=lax.Precision.HIGHEST)
    a = by_chunk(gt) - b
    lane = lax.broadcasted_iota(jnp.int32, a.shape, 1)
    amax = a
    shift = 1
    while shift < L:
        amax = jnp.maximum(amax, jnp.where(lane >= shift, pltpu.roll(amax, shift, axis=1), NEG_BIG))
        shift *= 2
    last = lambda x: jnp.broadcast_to(x[:, L - 1:L], x.shape)
    b_last, a_last = last(b), last(amax)
    m = jnp.zeros((R, L), F32)
    mu, m_prev, m_next = [], [], []
    for c in range(n_chunks):
        rows = slice(c * R, (c + 1) * R)
        mu.append(jnp.maximum(amax[rows], m))
        m_prev.append(m)
        m = b_last[rows] + jnp.maximum(a_last[rows], m)
        m_next.append(m)
    mu, m_prev, m_next = (jnp.concatenate(x, axis=0) for x in (mu, m_prev, m_next))
    w_inter = jnp.exp(m_prev - mu)
    inv_floor = jnp.exp(-(b + mu))
    w_k = jnp.exp(b_last + a - m_next)
    dec = jnp.exp(b_last + m_prev - m_next)
    pad = jnp.zeros((LANES - 4 * R, L), F32)
    for c in range(n_chunks):
        rows = slice(c * R, (c + 1) * R)
        a_ref[c] = a[rows]
        dec_ref[c] = dec[rows]
        col_ref[c] = jnp.concatenate([mu[rows], w_inter[rows], inv_floor[rows], w_k[rows], pad],
                                     axis=0).T

    state_ref[...] = jnp.zeros_like(state_ref)
    causal = t <= u
    ones_blk = jnp.ones((L, D), BF16)
    ones_sq = jnp.ones((D, D), BF16)

    def chunk(c, carry):
        t0 = pl.multiple_of(c * L, L)
        for h in range(H):
            hs = slice(h * D, (h + 1) * D)
            q = q_ref[0, pl.ds(t0, L), hs]
            k = k_ref[0, pl.ds(t0, L), hs]
            v = v_ref[0, pl.ds(t0, L), hs]
            a_row = a_ref[c, h:h + 1, :]
            decay = dec_ref[c, h:h + 1, :2 * D]
            mu_c, wi_c, fl_c, wk_c = (col_ref[c, :, j * R + h:j * R + h + 1] for j in range(4))
            state = state_ref[h]

            w = jnp.where(causal, jnp.exp(a_row - mu_c), 0.0)
            scores = (_dot_nt(q, k) * w).astype(BF16)
            nd = (_dot(scores, jnp.concatenate([v, ones_blk], axis=1))
                  + wi_c * _dot(q, state.astype(BF16)))
            hc = nd[:, :D] / jnp.maximum(jnp.abs(nd[:, D:]), fl_c)

            wv = jnp.concatenate([(wk_c * v.astype(F32)).astype(BF16),
                                  jnp.broadcast_to(wk_c, (L, D)).astype(BF16)], axis=1)
            state_ref[h] = decay * state + _dot_tn(k, wv)

            ms = _dot((hc * hc).astype(BF16), ones_sq) * (1.0 / D)
            hn = hc * lax.rsqrt(ms + EPS) * nw_ref[:, hs]
            gate = jax.nn.sigmoid(o_ref[0, pl.ds(t0, L), hs].astype(F32))
            out_ref[0, pl.ds(t0, L), hs] = (gate * hn).astype(BF16)
        return carry

    lax.fori_loop(0, n_chunks, chunk, 0)


def _mlstm(main, gates, gate_bias, norm_w):
    b, s, _ = main.shape
    w = MLSTM_WIDTH
    col = lambda c: pl.BlockSpec((1, s, w), lambda i, c=c: (i, 0, c))
    return pl.pallas_call(
        _mlstm_kernel,
        grid=(b,),
        in_specs=[
            col(0), col(1), col(2), col(3),
            pl.BlockSpec((1, s, GATE_COLS), lambda i: (i, 0, 0)),
            pl.BlockSpec((1, GATE_COLS), lambda i: (0, 0)),
            pl.BlockSpec((1, w), lambda i: (0, 0)),
        ],
        out_specs=pl.BlockSpec((1, s, w), lambda i: (i, 0, 0)),
        out_shape=jax.ShapeDtypeStruct((b, s, w), BF16),
        scratch_shapes=[
            pltpu.VMEM((MLSTM_HEADS, MLSTM_HEAD_DIM, 2 * MLSTM_HEAD_DIM), F32),
            pltpu.VMEM((s // MLSTM_CHUNK, SUBLANES, MLSTM_CHUNK), F32),
            pltpu.VMEM((s // MLSTM_CHUNK, SUBLANES, MLSTM_CHUNK), F32),
            pltpu.VMEM((s // MLSTM_CHUNK, MLSTM_CHUNK, LANES), F32),
        ],
        compiler_params=pltpu.CompilerParams(
            dimension_semantics=("arbitrary",), vmem_limit_bytes=VMEM_LIMIT),
        name="mlstm",
    )(main, main, main, main, gates, gate_bias, norm_w)


def _sb_kernel(q_ref, k_ref, v_ref, out_ref, acc_ref, carry_ref):
    T = SB_BLOCK
    pairs = SB_WIDTH // LANES
    lane = lax.broadcasted_iota(jnp.int32, (T, LANES), 1)
    head0 = lane < SB_HEAD_DIM
    row2 = lax.broadcasted_iota(jnp.int32, (2 * T, T), 0)
    col2 = lax.broadcasted_iota(jnp.int32, (2 * T, T), 1)
    strict = col2 < jnp.bitwise_and(row2, T - 1)
    row = lax.broadcasted_iota(jnp.int32, (T, T), 0)
    col = lax.broadcasted_iota(jnp.int32, (T, T), 1)
    later = (row > col).astype(BF16)

    def lanes(p):
        return slice(p * LANES, (p + 1) * LANES)

    def stacked_q(t0, p):
        q = q_ref[0, pl.ds(t0, T), lanes(p)]
        zero = jnp.zeros_like(q)
        return jnp.concatenate([jnp.where(head0, q, zero), jnp.where(head0, zero, q)], axis=0)

    def tile(qq, s0, p, shift, mask):
        k = k_ref[0, pl.ds(s0, T), lanes(p)]
        v = v_ref[0, pl.ds(s0, T), lanes(p)]
        z = _dot_nt(qq, k)
        neg_abs = lax.bitcast_convert_type(
            lax.bitcast_convert_type(z, jnp.uint32) | jnp.uint32(0x80000000), F32)
        t = jnp.minimum(z, 0.0) - jnp.log2(1.0 + jnp.exp2(neg_abs))
        sp = z - t
        if mask is not None:
            sp = jnp.where(mask, sp, 0.0)
        spb = sp.astype(BF16)
        r = _dot(spb, later)
        arg = t - r
        if shift is not None:
            arg = arg - shift
        a = jnp.exp2(arg)
        if mask is not None:
            a = jnp.where(mask, a, 0.0)
        o = _dot(a.astype(BF16), v)
        total = r[:, :1] + sp[:, :1]
        return o, total

    def merge_heads(o):
        return jnp.where(head0, o[:T], o[T:]).astype(BF16)

    for p in range(pairs):
        o, _ = tile(stacked_q(0, p), 0, p, None, strict)
        out_ref[0, 0:T, lanes(p)] = merge_heads(o)

    def q_block(i, carry):
        t0 = pl.multiple_of(i * T, T)
        t1 = pl.multiple_of(t0 - T, T)
        low = None
        for p in range(pairs):
            qq = stacked_q(t0, p)
            o0, tot0 = tile(qq, t0, p, None, strict)
            o1, tot1 = tile(qq, t1, p, tot0, None)
            acc_ref[p] = o0 + o1
            cp = tot0 + tot1
            carry_ref[p] = cp
            m = jnp.min(cp)
            low = m if low is None else jnp.minimum(low, m)

        def more(state):
            d, low = state
            return jnp.logical_and(d <= i, low < SB_SKIP_LOG2)

        def k_block(state):
            d, _ = state
            s0 = pl.multiple_of((i - d) * T, T)
            low = None
            for p in range(pairs):
                cp = carry_ref[p]
                o, tot = tile(stacked_q(t0, p), s0, p, cp, None)
                acc_ref[p] += o
                cp = cp + tot
                carry_ref[p] = cp
                m = jnp.min(cp)
                low = m if low is None else jnp.minimum(low, m)
            return d + 1, low

        lax.while_loop(more, k_block, (jnp.int32(2), low))
        for p in range(pairs):
            out_ref[0, pl.ds(t0, T), lanes(p)] = merge_heads(acc_ref[p])
        return carry

    lax.fori_loop(1, q_ref.shape[1] // T, q_block, 0)


def _sb_attn(main):
    b, s, _ = main.shape
    base = 4 * MLSTM_WIDTH // SB_WIDTH
    col = lambda off: pl.BlockSpec((1, s, SB_WIDTH), lambda i, off=off: (i, 0, base + off))
    return pl.pallas_call(
        _sb_kernel,
        grid=(b,),
        in_specs=[col(0), col(1), col(2)],
        out_specs=pl.BlockSpec((1, s, SB_WIDTH), lambda i: (i, 0, 0)),
        out_shape=jax.ShapeDtypeStruct((b, s, SB_WIDTH), BF16),
        scratch_shapes=[
            pltpu.VMEM((SB_WIDTH // LANES, 2 * SB_BLOCK, LANES), F32),
            pltpu.VMEM((SB_WIDTH // LANES, 2 * SB_BLOCK, 1), F32),
        ],
        compiler_params=pltpu.CompilerParams(
            dimension_semantics=("arbitrary",), vmem_limit_bytes=VMEM_LIMIT),
        name="sb_attn",
    )(main, main, main)


def _out_ffn_kernel(x_ref, hm_ref, hs_ref, wo_ref, g1_ref, g2_ref, wu_ref, cw_ref, cb_ref,
                    wd_ref, g3_ref, out_ref, x1_ref, h_ref, act_ref, buf_ref, halo_ref):
    @pl.when(pl.program_id(1) == 0)
    def _():
        halo_ref[...] = jnp.zeros_like(halo_ref)

    mix = _dot(hm_ref[0], wo_ref[:MLSTM_WIDTH, :]) + _dot(hs_ref[0], wo_ref[MLSTM_WIDTH:, :])
    x1 = x_ref[0] + _rms(mix, g1_ref[...])
    x1_ref[...] = x1
    h_ref[...] = _rms(x1, g2_ref[...]).astype(BF16)

    for c in range(D_FF // FF_CHUNK):
        lo, hi = c * FF_CHUNK, (c + 1) * FF_CHUNK
        h = h_ref[...]
        gate = _dot(h, wu_ref[:, lo:hi])
        up = _dot(h, wu_ref[:, D_FF + lo:D_FF + hi])
        taps = [cw_ref[k:k + 1, lo:hi] for k in range(FFN_CONV)]
        y = _causal_taps(buf_ref, halo_ref.at[c], gate, taps, FFN_CONV) + cb_ref[:, lo:hi]
        act_ref[:, lo:hi] = (jax.nn.gelu(y, approximate=True) * up).astype(BF16)

    y = _dot(act_ref[...], wd_ref[...])
    out_ref[0] = x1_ref[...] + _rms(y, g3_ref[...])


def _out_ffn(x, h_m, h_s, w_out, g1, g2, w_up, conv_w, conv_b, w_down, g3):
    b, s, d = x.shape
    const = lambda *_: (0, 0)
    tile = lambda w: pl.BlockSpec((1, SEQ_TILE, w), lambda i, j: (i, j, 0))
    resident = lambda shape: pl.BlockSpec(shape, const, pipeline_mode=pl.Buffered(1))
    return pl.pallas_call(
        _out_ffn_kernel,
        grid=(b, s // SEQ_TILE),
        in_specs=[
            tile(d), tile(MLSTM_WIDTH), tile(SB_WIDTH),
            resident((d, d)),
            pl.BlockSpec((1, d), const), pl.BlockSpec((1, d), const),
            resident((d, 2 * D_FF)),
            pl.BlockSpec((FFN_CONV, D_FF), const), pl.BlockSpec((1, D_FF), const),
            resident((D_FF, d)),
            pl.BlockSpec((1, d), const),
        ],
        out_specs=tile(d),
        out_shape=jax.ShapeDtypeStruct((b, s, d), F32),
        scratch_shapes=[
            pltpu.VMEM((SEQ_TILE, d), F32),
            pltpu.VMEM((SEQ_TILE, d), BF16),
            pltpu.VMEM((SEQ_TILE, D_FF), BF16),
            pltpu.VMEM((SEQ_TILE + SUBLANES, FF_CHUNK), F32),
            pltpu.VMEM((D_FF // FF_CHUNK, SUBLANES, FF_CHUNK), F32),
        ],
        compiler_params=pltpu.CompilerParams(
            dimension_semantics=("arbitrary", "arbitrary"), vmem_limit_bytes=VMEM_LIMIT),
        name="out_ffn",
    )(x, h_m, h_s, w_out, g1, g2, w_up, conv_w, conv_b, w_down, g3)


@jax.jit
def kernel(x, pre_mix_norm, w_in, mlstm_conv_w, mlstm_conv_b, mlstm_b_i, mlstm_b_f, mlstm_norm,
           w_out, post_mix_norm, pre_ffn_norm, w_up, ffn_conv_w, ffn_conv_b, w_down,
           post_ffn_norm):
    depth = w_in.shape[0]
    o_gate = 4 * MLSTM_WIDTH
    o_sb = o_gate + 2 * MLSTM_HEADS
    for l in range(depth):
        w = w_in[l]
        w_main = jnp.concatenate([w[:, :o_gate], w[:, o_sb:]], axis=1).astype(BF16)
        w_gate = jnp.pad(w[:, o_gate:o_sb], ((0, 0), (0, GATE_COLS - 2 * MLSTM_HEADS))).astype(BF16)
        gate_bias = jnp.pad(jnp.concatenate([mlstm_b_i[l], mlstm_b_f[l]]),
                            (0, GATE_COLS - 2 * MLSTM_HEADS))[None, :]
        main, gates = _in_proj(x, pre_mix_norm[l][None, :], w_main, w_gate,
                               mlstm_conv_w[l], mlstm_conv_b[l][None, :])
        h_m = _mlstm(main, gates, gate_bias, mlstm_norm[l][None, :])
        h_s = _sb_attn(main)
        x = _out_ffn(x, h_m, h_s, w_out[l].astype(BF16), post_mix_norm[l][None, :],
                     pre_ffn_norm[l][None, :], w_up[l].astype(BF16), ffn_conv_w[l],
                     ffn_conv_b[l][None, :], w_down[l].astype(BF16), post_ffn_norm[l][None, :])
    return x
```

```python
import functools

import jax
import jax.numpy as jnp
from jax import lax
from jax.experimental import pallas as pl
from jax.experimental.pallas import tpu as pltpu

D_MODEL = 1024
MLSTM_WIDTH = 512
MLSTM_HEADS = 4
MLSTM_HEAD_DIM = 128
MLSTM_CONV = 4
SB_WIDTH = 512
SB_HEADS = 8
SB_HEAD_DIM = 64
D_FF = 2816
FFN_CONV = 3
EPS = 1e-6

LANES = 128
SUBLANES = 8
MAIN_COLS = 4 * MLSTM_WIDTH + 3 * SB_WIDTH
GATE_COLS = LANES
PROJ_CHUNK = 256
SEQ_TILE = 512
MLSTM_CHUNK = 256
SB_BLOCK = 256
LOG2E = 1.4426950408889634
SB_SKIP_LOG2 = 151.0
FF_CHUNK = 256
FFN_SPLIT = 2
VMEM_LIMIT = 56 * 1024 * 1024
NEG_BIG = -1e30

F32 = jnp.float32
BF16 = jnp.bfloat16


def _rms(v, w):
    return v * lax.rsqrt(jnp.mean(v * v, axis=-1, keepdims=True) + EPS) * w


def _log_sigmoid(v):
    return jnp.minimum(v, 0.0) - jnp.log1p(jnp.exp(-jnp.abs(v)))


def _dot(a, b):
    return jnp.dot(a, b, preferred_element_type=F32)


def _dot_nt(a, b):
    return lax.dot_general(a, b, (((1,), (1,)), ((), ())), preferred_element_type=F32)


def _dot_tn(a, b):
    return lax.dot_general(a, b, (((0,), (0,)), ((), ())), preferred_element_type=F32)


def _causal_taps(buf_ref, halo_ref, p, taps, width):
    n = p.shape[0]
    buf_ref[0:SUBLANES, :] = halo_ref[...]
    buf_ref[SUBLANES:SUBLANES + n, :] = p
    halo_ref[...] = p[n - SUBLANES:, :]
    y = taps[width - 1] * p
    for j in range(1, width):
        y = y + taps[width - 1 - j] * buf_ref[SUBLANES - j:SUBLANES - j + n, :]
    return y


def _in_proj_kernel(x_ref, g_ref, wm_ref, ws_ref, wg_ref, cw_ref, cb_ref, main_ref, gate_ref,
                    h_ref, buf_ref, halo_ref):
    @pl.when(pl.program_id(1) == 0)
    def _():
        halo_ref[...] = jnp.zeros_like(halo_ref)

    h_slot = h_ref.at[pl.program_id(1) % 2]
    h_slot[...] = _rms(x_ref[0], g_ref[...]).astype(BF16)
    gate_ref[0] = _dot(h_slot[...], wg_ref[...])
    sb_q = 4 * MLSTM_WIDTH
    n_conv = 2 * MLSTM_WIDTH // PROJ_CHUNK
    n_all = MAIN_COLS // PROJ_CHUNK
    plain = list(range(n_conv, n_all))
    per = -(-len(plain) // n_conv)
    order = []
    for c in range(n_conv):
        order += [c] + plain[c * per:(c + 1) * per]
    for c in order:
        lo, hi = c * PROJ_CHUNK, (c + 1) * PROJ_CHUNK
        w = wm_ref[:, lo:hi] if hi <= sb_q else ws_ref[:, lo - sb_q:hi - sb_q]
        p = _dot(h_slot[...], w)
        if hi <= 2 * MLSTM_WIDTH:
            taps = [cw_ref[k:k + 1, lo:hi] for k in range(MLSTM_CONV)]
            y = _causal_taps(buf_ref, halo_ref.at[c], p, taps, MLSTM_CONV) + cb_ref[:, lo:hi]
            p = y * jax.nn.sigmoid(y)
            if hi <= MLSTM_WIDTH:
                p = p * MLSTM_HEAD_DIM ** -0.5
        elif sb_q <= lo and hi <= sb_q + SB_WIDTH:
            p = p * (SB_HEAD_DIM ** -0.5 * LOG2E)
        main_ref[0, :, lo:hi] = p.astype(BF16)


def _in_proj(x, g, w_mlstm, w_sb, w_gate, conv_w, conv_b):
    b, s, d = x.shape
    const = lambda *_: (0, 0)
    return pl.pallas_call(
        _in_proj_kernel,
        grid=(b, s // SEQ_TILE),
        in_specs=[
            pl.BlockSpec((1, SEQ_TILE, d), lambda i, j: (i, j, 0)),
            pl.BlockSpec((1, d), const),
            pl.BlockSpec((d, 4 * MLSTM_WIDTH), const),
            pl.BlockSpec((d, 3 * SB_WIDTH), const),
            pl.BlockSpec((d, GATE_COLS), const),
            pl.BlockSpec((MLSTM_CONV, 2 * MLSTM_WIDTH), const),
            pl.BlockSpec((1, 2 * MLSTM_WIDTH), const),
        ],
        out_specs=[
            pl.BlockSpec((1, SEQ_TILE, MAIN_COLS), lambda i, j: (i, j, 0)),
            pl.BlockSpec((1, SEQ_TILE, GATE_COLS), lambda i, j: (i, j, 0)),
        ],
        out_shape=[
            jax.ShapeDtypeStruct((b, s, MAIN_COLS), BF16),
            jax.ShapeDtypeStruct((b, s, GATE_COLS), F32),
        ],
        scratch_shapes=[
            pltpu.VMEM((2, SEQ_TILE, d), BF16),
            pltpu.VMEM((SEQ_TILE + SUBLANES, PROJ_CHUNK), F32),
            pltpu.VMEM((2 * MLSTM_WIDTH // PROJ_CHUNK, SUBLANES, PROJ_CHUNK), F32),
        ],
        compiler_params=pltpu.CompilerParams(
            dimension_semantics=("arbitrary", "arbitrary"), vmem_limit_bytes=VMEM_LIMIT),
        name="in_proj",
    )(x, g, w_mlstm, w_sb, w_gate, conv_w, conv_b)


def _mlstm_kernel(q_ref, k_ref, v_ref, o_ref, gate_ref, gb_ref, nw_ref, out_ref,
                  state_ref, a_ref, dec_ref, col_ref):
    L = MLSTM_CHUNK
    D = MLSTM_HEAD_DIM
    H = MLSTM_HEADS
    R = SUBLANES
    n_chunks = q_ref.shape[1] // L
    assert L >= 2 * D and 4 * R <= LANES

    gt = (gate_ref[0] + gb_ref[...]).T[:R, :]
    lf = _log_sigmoid(pltpu.roll(gt, H, axis=0))
    by_chunk = lambda x: jnp.concatenate([x[:, c * L:(c + 1) * L] for c in range(n_chunks)], axis=0)
    u = lax.broadcasted_iota(jnp.int32, (L, L), 0)
    t = lax.broadcasted_iota(jnp.int32, (L, L), 1)
    b = jnp.dot(by_chunk(lf), (u <= t).astype(F32), preferred_element_type=F32,
                precision=lax.Precision.HIGHEST)
    a = by_chunk(gt) - b
    lane = lax.broadcasted_iota(jnp.int32, a.shape, 1)
    amax = a
    shift = 1
    while shift < L:
        amax = jnp.maximum(amax, jnp.where(lane >= shift, pltpu.roll(amax, shift, axis=1), NEG_BIG))
        shift *= 2
    last = lambda x: jnp.broadcast_to(x[:, L - 1:L], x.shape)
    b_last, a_last = last(b), last(amax)
    m = jnp.zeros((R, L), F32)
    mu, m_prev, m_next = [], [], []
    for c in range(n_chunks):
        rows = slice(c * R, (c + 1) * R)
        mu.append(jnp.maximum(amax[rows], m))
        m_prev.append(m)
        m = b_last[rows] + jnp.maximum(a_last[rows], m)
        m_next.append(m)
    mu, m_prev, m_next = (jnp.concatenate(x, axis=0) for x in (mu, m_prev, m_next))
    w_inter = jnp.exp(m_prev - mu)
    inv_floor = jnp.exp(-(b + mu))
    w_k = jnp.exp(b_last + a - m_next)
    dec = jnp.exp(b_last + m_prev - m_next)
    pad = jnp.zeros((LANES - 4 * R, L), F32)
    for c in range(n_chunks):
        rows = slice(c * R, (c + 1) * R)
        a_ref[c] = a[rows]
        dec_ref[c] = dec[rows]
        col_ref[c] = jnp.concatenate([mu[rows], w_inter[rows], inv_floor[rows], w_k[rows], pad],
                                     axis=0).T

    state_ref[...] = jnp.zeros_like(state_ref)
    causal = t <= u
    ones_blk = jnp.ones((L, D), BF16)
    ones_sq = jnp.ones((D, D), BF16)

    def chunk(c, carry):
        t0 = pl.multiple_of(c * L, L)
        for h in range(H):
            hs = slice(h * D, (h + 1) * D)
            q = q_ref[0, pl.ds(t0, L), hs]
            k = k_ref[0, pl.ds(t0, L), hs]
            v = v_ref[0, pl.ds(t0, L), hs]
            a_row = a_ref[c, h:h + 1, :]
            decay = dec_ref[c, h:h + 1, :2 * D]
            mu_c, wi_c, fl_c, wk_c = (col_ref[c, :, j * R + h:j * R + h + 1] for j in range(4))
            state = state_ref[h]

            w = jnp.where(causal, jnp.exp(a_row - mu_c), 0.0)
            scores = (_dot_nt(q, k) * w).astype(BF16)
            nd = (_dot(scores, jnp.concatenate([v, ones_blk], axis=1))
                  + wi_c * _dot(q, state.astype(BF16)))
            hc = nd[:, :D] / jnp.maximum(jnp.abs(nd[:, D:]), fl_c)

            wv = jnp.concatenate([(wk_c * v.astype(F32)).astype(BF16),
                                  jnp.broadcast_to(wk_c, (L, D)).astype(BF16)], axis=1)
            state_ref[h] = decay * state + _dot_tn(k, wv)

            ms = _dot((hc * hc).astype(BF16), ones_sq) * (1.0 / D)
            hn = hc * lax.rsqrt(ms + EPS) * nw_ref[:, hs]
            gate = jax.nn.sigmoid(o_ref[0, pl.ds(t0, L), hs].astype(F32))
            out_ref[0, pl.ds(t0, L), hs] = (gate * hn).astype(BF16)
        return carry

    lax.fori_loop(0, n_chunks, chunk, 0, unroll=2)


def _mlstm(main, gates, gate_bias, norm_w):
    b, s, _ = main.shape
    w = MLSTM_WIDTH
    col = lambda c: pl.BlockSpec((1, s, w), lambda i, c=c: (i, 0, c))
    return pl.pallas_call(
        _mlstm_kernel,
        grid=(b,),
        in_specs=[
            col(0), col(1), col(2), col(3),
            pl.BlockSpec((1, s, GATE_COLS), lambda i: (i, 0, 0)),
            pl.BlockSpec((1, GATE_COLS), lambda i: (0, 0)),
            pl.BlockSpec((1, w), lambda i: (0, 0)),
        ],
        out_specs=pl.BlockSpec((1, s, w), lambda i: (i, 0, 0)),
        out_shape=jax.ShapeDtypeStruct((b, s, w), BF16),
        scratch_shapes=[
            pltpu.VMEM((MLSTM_HEADS, MLSTM_HEAD_DIM, 2 * MLSTM_HEAD_DIM), F32),
            pltpu.VMEM((s // MLSTM_CHUNK, SUBLANES, MLSTM_CHUNK), F32),
            pltpu.VMEM((s // MLSTM_CHUNK, SUBLANES, MLSTM_CHUNK), F32),
            pltpu.VMEM((s // MLSTM_CHUNK, MLSTM_CHUNK, LANES), F32),
        ],
        compiler_params=pltpu.CompilerParams(
            dimension_semantics=("arbitrary",), vmem_limit_bytes=VMEM_LIMIT),
        name="mlstm",
    )(main, main, main, main, gates, gate_bias, norm_w)


def _sb_kernel(q_ref, k_ref, v_ref, out_ref, acc_ref, carry_ref):
    T = SB_BLOCK
    pairs = SB_WIDTH // LANES
    lane = lax.broadcasted_iota(jnp.int32, (T, LANES), 1)
    head0 = lane < SB_HEAD_DIM
    row2 = lax.broadcasted_iota(jnp.int32, (2 * T, T), 0)
    col2 = lax.broadcasted_iota(jnp.int32, (2 * T, T), 1)
    strict = col2 < jnp.bitwise_and(row2, T - 1)
    row = lax.broadcasted_iota(jnp.int32, (T, T), 0)
    col = lax.broadcasted_iota(jnp.int32, (T, T), 1)
    later = (row > col).astype(BF16)

    def lanes(p):
        return slice(p * LANES, (p + 1) * LANES)

    def stacked_q(t0, p):
        q = q_ref[0, pl.ds(t0, T), lanes(p)]
        zero = jnp.zeros_like(q)
        return jnp.concatenate([jnp.where(head0, q, zero), jnp.where(head0, zero, q)], axis=0)

    def tile(qq, s0, p, shift, mask):
        k = k_ref[0, pl.ds(s0, T), lanes(p)]
        v = v_ref[0, pl.ds(s0, T), lanes(p)]
        z = _dot_nt(qq, k)
        neg_abs = lax.bitcast_convert_type(
            lax.bitcast_convert_type(z, jnp.uint32) | jnp.uint32(0x80000000), F32)
        t = jnp.minimum(z, 0.0) - jnp.log2(1.0 + jnp.exp2(neg_abs))
        sp = z - t
        if mask is not None:
            sp = jnp.where(mask, sp, 0.0)
        r = _dot(sp.astype(BF16), later)
        arg = t - r
        if shift is not None:
            arg = arg - shift
        a = jnp.exp2(arg)
        if mask is not None:
            a = jnp.where(mask, a, 0.0)
        o = _dot(a.astype(BF16), v)
        total = r[:, :1] + sp[:, :1]
        return o, total

    def merge_heads(o):
        return jnp.where(head0, o[:T], o[T:]).astype(BF16)

    for p in range(pairs):
        o, _ = tile(stacked_q(0, p), 0, p, None, strict)
        out_ref[0, 0:T, lanes(p)] = merge_heads(o)

    def q_block(i, carry):
        t0 = pl.multiple_of(i * T, T)
        t1 = pl.multiple_of(t0 - T, T)
        low = None
        for p in range(pairs):
            qq = stacked_q(t0, p)
            o0, tot0 = tile(qq, t0, p, None, strict)
            o1, tot1 = tile(qq, t1, p, tot0, None)
            acc_ref[p] = o0 + o1
            cp = tot0 + tot1
            carry_ref[p] = cp
            m = jnp.min(cp)
            low = m if low is None else jnp.minimum(low, m)

        def more(state):
            d, low = state
            return jnp.logical_and(d <= i, low < SB_SKIP_LOG2)

        def k_block(state):
            d, _ = state
            s0 = pl.multiple_of((i - d) * T, T)
            low = None
            for p in range(pairs):
                cp = carry_ref[p]
                o, tot = tile(stacked_q(t0, p), s0, p, cp, None)
                acc_ref[p] += o
                cp = cp + tot
                carry_ref[p] = cp
                m = jnp.min(cp)
                low = m if low is None else jnp.minimum(low, m)
            return d + 1, low

        lax.while_loop(more, k_block, (jnp.int32(2), low))
        for p in range(pairs):
            out_ref[0, pl.ds(t0, T), lanes(p)] = merge_heads(acc_ref[p])
        return carry

    lax.fori_loop(1, q_ref.shape[1] // T, q_block, 0)


def _sb_attn(main):
    b, s, _ = main.shape
    base = 4 * MLSTM_WIDTH // SB_WIDTH
    col = lambda off: pl.BlockSpec((1, s, SB_WIDTH), lambda i, off=off: (i, 0, base + off))
    return pl.pallas_call(
        _sb_kernel,
        grid=(b,),
        in_specs=[col(0), col(1), col(2)],
        out_specs=pl.BlockSpec((1, s, SB_WIDTH), lambda i: (i, 0, 0)),
        out_shape=jax.ShapeDtypeStruct((b, s, SB_WIDTH), BF16),
        scratch_shapes=[
            pltpu.VMEM((SB_WIDTH // LANES, 2 * SB_BLOCK, LANES), F32),
            pltpu.VMEM((SB_WIDTH // LANES, 2 * SB_BLOCK, 1), F32),
        ],
        compiler_params=pltpu.CompilerParams(
            dimension_semantics=("arbitrary",), vmem_limit_bytes=VMEM_LIMIT),
        name="sb_attn",
    )(main, main, main)


def _out_ffn_kernel(x_ref, hm_ref, hs_ref, wo_ref, g1_ref, g2_ref, wu_ref, cw_ref, cb_ref,
                    wd_ref, g3_ref, out_ref, x1_ref, h_ref, act_ref, buf_ref, halo_ref):
    @pl.when(pl.program_id(1) == 0)
    def _():
        halo_ref[...] = jnp.zeros_like(halo_ref)

    sub = SEQ_TILE // FFN_SPLIT
    for r in range(FFN_SPLIT):
        rows = slice(r * sub, (r + 1) * sub)
        mix = (_dot(hm_ref[0, rows, :], wo_ref[:MLSTM_WIDTH, :])
               + _dot(hs_ref[0, rows, :], wo_ref[MLSTM_WIDTH:, :]))
        x1 = x_ref[0, rows, :] + _rms(mix, g1_ref[...])
        x1_ref[rows, :] = x1
        h_ref[rows, :] = _rms(x1, g2_ref[...]).astype(BF16)

    for r in range(FFN_SPLIT):
        rows = slice(r * sub, (r + 1) * sub)
        for c in range(D_FF // FF_CHUNK):
            lo, hi = c * FF_CHUNK, (c + 1) * FF_CHUNK
            h = h_ref[rows, :]
            gate = _dot(h, wu_ref[:, lo:hi])
            up = _dot(h, wu_ref[:, D_FF + lo:D_FF + hi])
            taps = [cw_ref[k:k + 1, lo:hi] for k in range(FFN_CONV)]
            y = _causal_taps(buf_ref.at[r], halo_ref.at[c], gate, taps, FFN_CONV) + cb_ref[:, lo:hi]
            act_ref[rows, lo:hi] = (jax.nn.gelu(y, approximate=True) * up).astype(BF16)
        y = _dot(act_ref[rows, :], wd_ref[...])
        out_ref[0, rows, :] = x1_ref[rows, :] + _rms(y, g3_ref[...])


def _out_ffn(x, h_m, h_s, w_out, g1, g2, w_up, conv_w, conv_b, w_down, g3):
    b, s, d = x.shape
    const = lambda *_: (0, 0)
    tile = lambda w: pl.BlockSpec((1, SEQ_TILE, w), lambda i, j: (i, j, 0))
    resident = lambda shape: pl.BlockSpec(shape, const, pipeline_mode=pl.Buffered(1))
    return pl.pallas_call(
        _out_ffn_kernel,
        grid=(b, s // SEQ_TILE),
        in_specs=[
            tile(d), tile(MLSTM_WIDTH), tile(SB_WIDTH),
            resident((d, d)),
            pl.BlockSpec((1, d), const), pl.BlockSpec((1, d), const),
            resident((d, 2 * D_FF)),
            pl.BlockSpec((FFN_CONV, D_FF), const), pl.BlockSpec((1, D_FF), const),
            resident((D_FF, d)),
            pl.BlockSpec((1, d), const),
        ],
        out_specs=tile(d),
        out_shape=jax.ShapeDtypeStruct((b, s, d), F32),
        scratch_shapes=[
            pltpu.VMEM((SEQ_TILE, d), F32),
            pltpu.VMEM((SEQ_TILE, d), BF16),
            pltpu.VMEM((SEQ_TILE, D_FF), BF16),
            pltpu.VMEM((FFN_SPLIT, SEQ_TILE // FFN_SPLIT + SUBLANES, FF_CHUNK), F32),
            pltpu.VMEM((D_FF // FF_CHUNK, SUBLANES, FF_CHUNK), F32),
        ],
        compiler_params=pltpu.CompilerParams(
            dimension_semantics=("arbitrary", "arbitrary"), vmem_limit_bytes=VMEM_LIMIT),
        name="out_ffn",
    )(x, h_m, h_s, w_out, g1, g2, w_up, conv_w, conv_b, w_down, g3)


@jax.jit
def kernel(x, pre_mix_norm, w_in, mlstm_conv_w, mlstm_conv_b, mlstm_b_i, mlstm_b_f, mlstm_norm,
           w_out, post_mix_norm, pre_ffn_norm, w_up, ffn_conv_w, ffn_conv_b, w_down,
           post_ffn_norm):
    depth = w_in.shape[0]
    o_gate = 4 * MLSTM_WIDTH
    o_sb = o_gate + 2 * MLSTM_HEADS
    for l in range(depth):
        w = w_in[l].astype(BF16)
        w_gate = jnp.pad(w[:, o_gate:o_sb], ((0, 0), (0, GATE_COLS - 2 * MLSTM_HEADS)))
        gate_bias = jnp.pad(jnp.concatenate([mlstm_b_i[l], mlstm_b_f[l]]),
                            (0, GATE_COLS - 2 * MLSTM_HEADS))[None, :]
        main, gates = _in_proj(x, pre_mix_norm[l][None, :], w[:, :o_gate], w[:, o_sb:], w_gate,
                               mlstm_conv_w[l], mlstm_conv_b[l][None, :])
        h_m = _mlstm(main, gates, gate_bias, mlstm_norm[l][None, :])
        h_s = _sb_attn(main)
        x = _out_ffn(x, h_m, h_s, w_out[l].astype(BF16), post_mix_norm[l][None, :],
                     pre_ffn_norm[l][None, :], w_up[l].astype(BF16), ffn_conv_w[l],
                     ffn_conv_b[l][None, :], w_down[l].astype(BF16), post_ffn_norm[l][None, :])
    return x
```

```python
import functools

import jax
import jax.numpy as jnp
from jax import lax
from jax.experimental import pallas as pl
from jax.experimental.pallas import tpu as pltpu

D_MODEL = 1024
MLSTM_WIDTH = 512
MLSTM_HEADS = 4
MLSTM_HEAD_DIM = 128
MLSTM_CONV = 4
SB_WIDTH = 512
SB_HEADS = 8
SB_HEAD_DIM = 64
D_FF = 2816
FFN_CONV = 3
EPS = 1e-6

LANES = 128
SUBLANES = 8
MAIN_COLS = 4 * MLSTM_WIDTH + 3 * SB_WIDTH
GATE_COLS = LANES
PROJ_CHUNK = 256
SEQ_TILE = 1024
MLSTM_CHUNK = 256
SB_BLOCK = 256
LOG2E = 1.4426950408889634
SB_SKIP_LOG2 = 151.0
FF_CHUNK = 256
FFN_SPLIT = 4
VMEM_LIMIT = 56 * 1024 * 1024
NEG_BIG = -1e30

F32 = jnp.float32
BF16 = jnp.bfloat16


def _rms(v, w):
    return v * lax.rsqrt(jnp.mean(v * v, axis=-1, keepdims=True) + EPS) * w


def _log_sigmoid(v):
    return jnp.minimum(v, 0.0) - jnp.log1p(jnp.exp(-jnp.abs(v)))


def _dot(a, b):
    return jnp.dot(a, b, preferred_element_type=F32)


def _dot_nt(a, b):
    return lax.dot_general(a, b, (((1,), (1,)), ((), ())), preferred_element_type=F32)


def _dot_tn(a, b):
    return lax.dot_general(a, b, (((0,), (0,)), ((), ())), preferred_element_type=F32)


def _causal_taps(buf_ref, halo_ref, p, taps, width):
    n = p.shape[0]
    buf_ref[0:SUBLANES, :] = halo_ref[...]
    buf_ref[SUBLANES:SUBLANES + n, :] = p
    halo_ref[...] = p[n - SUBLANES:, :]
    y = taps[width - 1] * p
    for j in range(1, width):
        y = y + taps[width - 1 - j] * buf_ref[SUBLANES - j:SUBLANES - j + n, :]
    return y


def _in_proj_kernel(x_ref, g_ref, wm_ref, ws_ref, wg_ref, cw_ref, cb_ref, main_ref, gate_ref,
                    h_ref, buf_ref, halo_ref):
    @pl.when(pl.program_id(1) == 0)
    def _():
        halo_ref[...] = jnp.zeros_like(halo_ref)

    h_slot = h_ref.at[pl.program_id(1) % 2]
    h_slot[...] = _rms(x_ref[0], g_ref[...]).astype(BF16)
    gate_ref[0] = _dot(h_slot[...], wg_ref[...])
    sb_q = 4 * MLSTM_WIDTH
    n_conv = 2 * MLSTM_WIDTH // PROJ_CHUNK
    n_all = MAIN_COLS // PROJ_CHUNK
    plain = list(range(n_conv, n_all))
    per = -(-len(plain) // n_conv)
    order = []
    for c in range(n_conv):
        order += [c] + plain[c * per:(c + 1) * per]
    for c in order:
        lo, hi = c * PROJ_CHUNK, (c + 1) * PROJ_CHUNK
        w = wm_ref[:, lo:hi] if hi <= sb_q else ws_ref[:, lo - sb_q:hi - sb_q]
        p = _dot(h_slot[...], w)
        if hi <= 2 * MLSTM_WIDTH:
            taps = [cw_ref[k:k + 1, lo:hi] for k in range(MLSTM_CONV)]
            y = _causal_taps(buf_ref, halo_ref.at[c], p, taps, MLSTM_CONV) + cb_ref[:, lo:hi]
            p = y * jax.nn.sigmoid(y)
            if hi <= MLSTM_WIDTH:
                p = p * MLSTM_HEAD_DIM ** -0.5
        elif sb_q <= lo and hi <= sb_q + SB_WIDTH:
            p = p * (SB_HEAD_DIM ** -0.5 * LOG2E)
        main_ref[0, :, lo:hi] = p.astype(BF16)


def _in_proj(x, g, w_mlstm, w_sb, w_gate, conv_w, conv_b):
    b, s, d = x.shape
    const = lambda *_: (0, 0)
    return pl.pallas_call(
        _in_proj_kernel,
        grid=(b, s // SEQ_TILE),
        in_specs=[
            pl.BlockSpec((1, SEQ_TILE, d), lambda i, j: (i, j, 0)),
            pl.BlockSpec((1, d), const),
            pl.BlockSpec((d, 4 * MLSTM_WIDTH), const),
            pl.BlockSpec((d, 3 * SB_WIDTH), const),
            pl.BlockSpec((d, GATE_COLS), const),
            pl.BlockSpec((MLSTM_CONV, 2 * MLSTM_WIDTH), const),
            pl.BlockSpec((1, 2 * MLSTM_WIDTH), const),
        ],
        out_specs=[
            pl.BlockSpec((1, SEQ_TILE, MAIN_COLS), lambda i, j: (i, j, 0)),
            pl.BlockSpec((1, SEQ_TILE, GATE_COLS), lambda i, j: (i, j, 0)),
        ],
        out_shape=[
            jax.ShapeDtypeStruct((b, s, MAIN_COLS), BF16),
            jax.ShapeDtypeStruct((b, s, GATE_COLS), F32),
        ],
        scratch_shapes=[
            pltpu.VMEM((2, SEQ_TILE, d), BF16),
            pltpu.VMEM((SEQ_TILE + SUBLANES, PROJ_CHUNK), F32),
            pltpu.VMEM((2 * MLSTM_WIDTH // PROJ_CHUNK, SUBLANES, PROJ_CHUNK), F32),
        ],
        compiler_params=pltpu.CompilerParams(
            dimension_semantics=("arbitrary", "arbitrary"), vmem_limit_bytes=VMEM_LIMIT),
        name="in_proj",
    )(x, g, w_mlstm, w_sb, w_gate, conv_w, conv_b)


def _mlstm_kernel(q_ref, k_ref, v_ref, o_ref, gate_ref, gb_ref, nw_ref, out_ref,
                  state_ref, a_ref, dec_ref, col_ref):
    L = MLSTM_CHUNK
    D = MLSTM_HEAD_DIM
    H = MLSTM_HEADS
    R = SUBLANES
    n_chunks = q_ref.shape[1] // L
    assert L >= 2 * D and 4 * R <= LANES

    gt = (gate_ref[0] + gb_ref[...]).T[:R, :]
    lf = _log_sigmoid(pltpu.roll(gt, H, axis=0))
    by_chunk = lambda x: jnp.concatenate([x[:, c * L:(c + 1) * L] for c in range(n_chunks)], axis=0)
    u = lax.broadcasted_iota(jnp.int32, (L, L), 0)
    t = lax.broadcasted_iota(jnp.int32, (L, L), 1)
    b = jnp.dot(by_chunk(lf), (u <= t).astype(F32), preferred_element_type=F32,
                precision=lax.Precision.HIGHEST)
    a = by_chunk(gt) - b
    lane = lax.broadcasted_iota(jnp.int32, a.shape, 1)
    amax = a
    shift = 1
    while shift < L:
        amax = jnp.maximum(amax, jnp.where(lane >= shift, pltpu.roll(amax, shift, axis=1), NEG_BIG))
        shift *= 2
    last = lambda x: jnp.broadcast_to(x[:, L - 1:L], x.shape)
    b_last, a_last = last(b), last(amax)
    m = jnp.zeros((R, L), F32)
    mu, m_prev, m_next = [], [], []
    for c in range(n_chunks):
        rows = slice(c * R, (c + 1) * R)
        mu.append(jnp.maximum(amax[rows], m))
        m_prev.append(m)
        m = b_last[rows] + jnp.maximum(a_last[rows], m)
        m_next.append(m)
    mu, m_prev, m_next = (jnp.concatenate(x, axis=0) for x in (mu, m_prev, m_next))
    w_inter = jnp.exp(m_prev - mu)
    inv_floor = jnp.exp(-(b + mu))
    w_k = jnp.exp(b_last + a - m_next)
    dec = jnp.exp(b_last + m_prev - m_next)
    pad = jnp.zeros((LANES - 4 * R, L), F32)
    for c in range(n_chunks):
        rows = slice(c * R, (c + 1) * R)
        a_ref[c] = a[rows]
        dec_ref[c] = dec[rows]
        col_ref[c] = jnp.concatenate([mu[rows], w_inter[rows], inv_floor[rows], w_k[rows], pad],
                                     axis=0).T

    state_ref[...] = jnp.zeros_like(state_ref)
    causal = t <= u
    ones_blk = jnp.ones((L, D), BF16)
    ones_sq = jnp.ones((D, D), BF16)

    def chunk(c, carry):
        t0 = pl.multiple_of(c * L, L)
        for h in range(H):
            hs = slice(h * D, (h + 1) * D)
            q = q_ref[0, pl.ds(t0, L), hs]
            k = k_ref[0, pl.ds(t0, L), hs]
            v = v_ref[0, pl.ds(t0, L), hs]
            a_row = a_ref[c, h:h + 1, :]
            decay = dec_ref[c, h:h + 1, :2 * D]
            mu_c, wi_c, fl_c, wk_c = (col_ref[c, :, j * R + h:j * R + h + 1] for j in range(4))
            state = state_ref[h]

            w = jnp.where(causal, jnp.exp(a_row - mu_c), 0.0)
            scores = (_dot_nt(q, k) * w).astype(BF16)
            nd = (_dot(scores, jnp.concatenate([v, ones_blk], axis=1))
                  + wi_c * _dot(q, state.astype(BF16)))
            hc = nd[:, :D] / jnp.maximum(jnp.abs(nd[:, D:]), fl_c)

            wv = jnp.concatenate([(wk_c * v.astype(F32)).astype(BF16),
                                  jnp.broadcast_to(wk_c, (L, D)).astype(BF16)], axis=1)
            state_ref[h] = decay * state + _dot_tn(k, wv)

            ms = _dot((hc * hc).astype(BF16), ones_sq) * (1.0 / D)
            hn = hc * lax.rsqrt(ms + EPS) * nw_ref[:, hs]
            gate = jax.nn.sigmoid(o_ref[0, pl.ds(t0, L), hs].astype(F32))
            out_ref[0, pl.ds(t0, L), hs] = (gate * hn).astype(BF16)
        return carry

    lax.fori_loop(0, n_chunks, chunk, 0, unroll=4)


def _mlstm(main, gates, gate_bias, norm_w):
    b, s, _ = main.shape
    w = MLSTM_WIDTH
    col = lambda c: pl.BlockSpec((1, s, w), lambda i, c=c: (i, 0, c))
    return pl.pallas_call(
        _mlstm_kernel,
        grid=(b,),
        in_specs=[
            col(0), col(1), col(2), col(3),
            pl.BlockSpec((1, s, GATE_COLS), lambda i: (i, 0, 0)),
            pl.BlockSpec((1, GATE_COLS), lambda i: (0, 0)),
            pl.BlockSpec((1, w), lambda i: (0, 0)),
        ],
        out_specs=pl.BlockSpec((1, s, w), lambda i: (i, 0, 0)),
        out_shape=jax.ShapeDtypeStruct((b, s, w), BF16),
        scratch_shapes=[
            pltpu.VMEM((MLSTM_HEADS, MLSTM_HEAD_DIM, 2 * MLSTM_HEAD_DIM), F32),
            pltpu.VMEM((s // MLSTM_CHUNK, SUBLANES, MLSTM_CHUNK), F32),
            pltpu.VMEM((s // MLSTM_CHUNK, SUBLANES, MLSTM_CHUNK), F32),
            pltpu.VMEM((s // MLSTM_CHUNK, MLSTM_CHUNK, LANES), F32),
        ],
        compiler_params=pltpu.CompilerParams(
            dimension_semantics=("arbitrary",), vmem_limit_bytes=VMEM_LIMIT),
        name="mlstm",
    )(main, main, main, main, gates, gate_bias, norm_w)


def _sb_kernel(q_ref, k_ref, v_ref, out_ref, acc_ref, carry_ref):
    T = SB_BLOCK
    pairs = SB_WIDTH // LANES
    lane = lax.broadcasted_iota(jnp.int32, (T, LANES), 1)
    head0 = lane < SB_HEAD_DIM
    row2 = lax.broadcasted_iota(jnp.int32, (2 * T, T), 0)
    col2 = lax.broadcasted_iota(jnp.int32, (2 * T, T), 1)
    strict = col2 < jnp.bitwise_and(row2, T - 1)
    row = lax.broadcasted_iota(jnp.int32, (T, T), 0)
    col = lax.broadcasted_iota(jnp.int32, (T, T), 1)
    later = (row > col).astype(BF16)

    def lanes(p):
        return slice(p * LANES, (p + 1) * LANES)

    def stacked_q(t0, p):
        q = q_ref[0, pl.ds(t0, T), lanes(p)]
        zero = jnp.zeros_like(q)
        return jnp.concatenate([jnp.where(head0, q, zero), jnp.where(head0, zero, q)], axis=0)

    def tile(qq, s0, p, shift, mask):
        k = k_ref[0, pl.ds(s0, T), lanes(p)]
        v = v_ref[0, pl.ds(s0, T), lanes(p)]
        z = _dot_nt(qq, k)
        neg_abs = lax.bitcast_convert_type(
            lax.bitcast_convert_type(z, jnp.uint32) | jnp.uint32(0x80000000), F32)
        t = jnp.minimum(z, 0.0) - jnp.log2(1.0 + jnp.exp2(neg_abs))
        sp = z - t
        if mask is not None:
            sp = jnp.where(mask, sp, 0.0)
        r = _dot(sp.astype(BF16), later)
        arg = t - r
        if shift is not None:
            arg = arg - shift
        a = jnp.exp2(arg)
        if mask is not None:
            a = jnp.where(mask, a, 0.0)
        o = _dot(a.astype(BF16), v)
        total = r[:, :1] + sp[:, :1]
        return o, total

    def merge_heads(o):
        return jnp.where(head0, o[:T], o[T:]).astype(BF16)

    for p in range(pairs):
        o, _ = tile(stacked_q(0, p), 0, p, None, strict)
        out_ref[0, 0:T, lanes(p)] = merge_heads(o)

    def q_block(i, carry):
        t0 = pl.multiple_of(i * T, T)
        t1 = pl.multiple_of(t0 - T, T)
        low = None
        for p in range(pairs):
            qq = stacked_q(t0, p)
            o0, tot0 = tile(qq, t0, p, None, strict)
            o1, tot1 = tile(qq, t1, p, tot0, None)
            acc_ref[p] = o0 + o1
            cp = tot0 + tot1
            carry_ref[p] = cp
            m = jnp.min(cp)
            low = m if low is None else jnp.minimum(low, m)

        def more(state):
            d, low = state
            return jnp.logical_and(d <= i, low < SB_SKIP_LOG2)

        def k_block(state):
            d, _ = state
            s0 = pl.multiple_of((i - d) * T, T)
            low = None
            for p in range(pairs):
                cp = carry_ref[p]
                o, tot = tile(stacked_q(t0, p), s0, p, cp, None)
                acc_ref[p] += o
                cp = cp + tot
                carry_ref[p] = cp
                m = jnp.min(cp)
                low = m if low is None else jnp.minimum(low, m)
            return d + 1, low

        lax.while_loop(more, k_block, (jnp.int32(2), low))
        for p in range(pairs):
            out_ref[0, pl.ds(t0, T), lanes(p)] = merge_heads(acc_ref[p])
        return carry

    lax.fori_loop(1, q_ref.shape[1] // T, q_block, 0)


def _sb_attn(main):
    b, s, _ = main.shape
    base = 4 * MLSTM_WIDTH // SB_WIDTH
    col = lambda off: pl.BlockSpec((1, s, SB_WIDTH), lambda i, off=off: (i, 0, base + off))
    return pl.pallas_call(
        _sb_kernel,
        grid=(b,),
        in_specs=[col(0), col(1), col(2)],
        out_specs=pl.BlockSpec((1, s, SB_WIDTH), lambda i: (i, 0, 0)),
        out_shape=jax.ShapeDtypeStruct((b, s, SB_WIDTH), BF16),
        scratch_shapes=[
            pltpu.VMEM((SB_WIDTH // LANES, 2 * SB_BLOCK, LANES), F32),
            pltpu.VMEM((SB_WIDTH // LANES, 2 * SB_BLOCK, 1), F32),
        ],
        compiler_params=pltpu.CompilerParams(
            dimension_semantics=("arbitrary",), vmem_limit_bytes=VMEM_LIMIT),
        name="sb_attn",
    )(main, main, main)


def _out_ffn_kernel(x_ref, hm_ref, hs_ref, wo_ref, g1_ref, g2_ref, wu_ref, cw_ref, cb_ref,
                    wd_ref, g3_ref, out_ref, x1_ref, h_ref, act_ref, buf_ref, halo_ref):
    @pl.when(pl.program_id(1) == 0)
    def _():
        halo_ref[...] = jnp.zeros_like(halo_ref)

    sub = SEQ_TILE // FFN_SPLIT
    for r in range(FFN_SPLIT):
        rows = slice(r * sub, (r + 1) * sub)
        mix = (_dot(hm_ref[0, rows, :], wo_ref[:MLSTM_WIDTH, :])
               + _dot(hs_ref[0, rows, :], wo_ref[MLSTM_WIDTH:, :]))
        x1 = x_ref[0, rows, :] + _rms(mix, g1_ref[...])
        x1_ref[rows, :] = x1
        h_ref[rows, :] = _rms(x1, g2_ref[...]).astype(BF16)

    for r in range(FFN_SPLIT):
        rows = slice(r * sub, (r + 1) * sub)
        for c in range(D_FF // FF_CHUNK):
            lo, hi = c * FF_CHUNK, (c + 1) * FF_CHUNK
            h = h_ref[rows, :]
            gate = _dot(h, wu_ref[:, lo:hi])
            up = _dot(h, wu_ref[:, D_FF + lo:D_FF + hi])
            taps = [cw_ref[k:k + 1, lo:hi] for k in range(FFN_CONV)]
            y = _causal_taps(buf_ref.at[r], halo_ref.at[c], gate, taps, FFN_CONV) + cb_ref[:, lo:hi]
            act_ref[rows, lo:hi] = (jax.nn.gelu(y, approximate=True) * up).astype(BF16)
        y = _dot(act_ref[rows, :], wd_ref[...])
        out_ref[0, rows, :] = x1_ref[rows, :] + _rms(y, g3_ref[...])


def _out_ffn(x, h_m, h_s, w_out, g1, g2, w_up, conv_w, conv_b, w_down, g3):
    b, s, d = x.shape
    const = lambda *_: (0, 0)
    tile = lambda w: pl.BlockSpec((1, SEQ_TILE, w), lambda i, j: (i, j, 0))
    resident = lambda shape: pl.BlockSpec(shape, const, pipeline_mode=pl.Buffered(1))
    return pl.pallas_call(
        _out_ffn_kernel,
        grid=(b, s // SEQ_TILE),
        in_specs=[
            tile(d), tile(MLSTM_WIDTH), tile(SB_WIDTH),
            resident((d, d)),
            pl.BlockSpec((1, d), const), pl.BlockSpec((1, d), const),
            resident((d, 2 * D_FF)),
            pl.BlockSpec((FFN_CONV, D_FF), const), pl.BlockSpec((1, D_FF), const),
            resident((D_FF, d)),
            pl.BlockSpec((1, d), const),
        ],
        out_specs=tile(d),
        out_shape=jax.ShapeDtypeStruct((b, s, d), F32),
        scratch_shapes=[
            pltpu.VMEM((SEQ_TILE, d), F32),
            pltpu.VMEM((SEQ_TILE, d), BF16),
            pltpu.VMEM((SEQ_TILE, D_FF), BF16),
            pltpu.VMEM((FFN_SPLIT, SEQ_TILE // FFN_SPLIT + SUBLANES, FF_CHUNK), F32),
            pltpu.VMEM((D_FF // FF_CHUNK, SUBLANES, FF_CHUNK), F32),
        ],
        compiler_params=pltpu.CompilerParams(
            dimension_semantics=("arbitrary", "arbitrary"), vmem_limit_bytes=VMEM_LIMIT),
        name="out_ffn",
    )(x, h_m, h_s, w_out, g1, g2, w_up, conv_w, conv_b, w_down, g3)


@jax.jit
def kernel(x, pre_mix_norm, w_in, mlstm_conv_w, mlstm_conv_b, mlstm_b_i, mlstm_b_f, mlstm_norm,
           w_out, post_mix_norm, pre_ffn_norm, w_up, ffn_conv_w, ffn_conv_b, w_down,
           post_ffn_norm):
    depth = w_in.shape[0]
    o_gate = 4 * MLSTM_WIDTH
    o_sb = o_gate + 2 * MLSTM_HEADS
    for l in range(depth):
        w = w_in[l]
        w_gate = jnp.pad(w[:, o_gate:o_sb].astype(BF16), ((0, 0), (0, GATE_COLS - 2 * MLSTM_HEADS)))
        gate_bias = jnp.pad(jnp.concatenate([mlstm_b_i[l], mlstm_b_f[l]]),
                            (0, GATE_COLS - 2 * MLSTM_HEADS))[None, :]
        main, gates = _in_proj(x, pre_mix_norm[l][None, :], w[:, :o_gate].astype(BF16),
                               w[:, o_sb:].astype(BF16), w_gate,
                               mlstm_conv_w[l], mlstm_conv_b[l][None, :])
        h_m = _mlstm(main, gates, gate_bias, mlstm_norm[l][None, :])
        h_s = _sb_attn(main)
        x = _out_ffn(x, h_m, h_s, w_out[l].astype(BF16), post_mix_norm[l][None, :],
                     pre_ffn_norm[l][None, :], w_up[l].astype(BF16), ffn_conv_w[l],
                     ffn_conv_b[l][None, :], w_down[l].astype(BF16), post_ffn_norm[l][None, :])
    return x
```

```python
import jax
import jax.numpy as jnp
from jax import lax
from jax.experimental import pallas as pl
from jax.experimental.pallas import tpu as pltpu

D_MODEL = 1024
MLSTM_WIDTH = 512
MLSTM_HEADS = 4
MLSTM_HEAD_DIM = 128
MLSTM_CONV = 4
SB_WIDTH = 512
SB_HEADS = 8
SB_HEAD_DIM = 64
D_FF = 2816
FFN_CONV = 3
EPS = 1e-6

LANES = 128
SUBLANES = 8
MAIN_COLS = 4 * MLSTM_WIDTH + 3 * SB_WIDTH
GATE_COLS = LANES
PROJ_CHUNK = 256
SEQ_TILE = 1024
MLSTM_CHUNK = 256
SB_BLOCK = 256
LOG2E = 1.4426950408889634
SB_SKIP_LOG2 = 151.0
FF_CHUNK = 256
FFN_SPLIT = 4
VMEM_LIMIT = 56 * 1024 * 1024
NEG_BIG = -1e30

F32 = jnp.float32
BF16 = jnp.bfloat16


def _rms(v, w):
    return v * lax.rsqrt(jnp.mean(v * v, axis=-1, keepdims=True) + EPS) * w


def _log_sigmoid(v):
    return jnp.minimum(v, 0.0) - jnp.log1p(jnp.exp(-jnp.abs(v)))


def _dot(a, b):
    return jnp.dot(a, b, preferred_element_type=F32)


def _dot_nt(a, b):
    return lax.dot_general(a, b, (((1,), (1,)), ((), ())), preferred_element_type=F32)


def _dot_tn(a, b):
    return lax.dot_general(a, b, (((0,), (0,)), ((), ())), preferred_element_type=F32)


def _causal_taps(buf_ref, halo_ref, p, taps, width):
    n = p.shape[0]
    buf_ref[0:SUBLANES, :] = halo_ref[...]
    buf_ref[SUBLANES:SUBLANES + n, :] = p
    halo_ref[...] = p[n - SUBLANES:, :]
    y = taps[width - 1] * p
    for j in range(1, width):
        y = y + taps[width - 1 - j] * buf_ref[SUBLANES - j:SUBLANES - j + n, :]
    return y


def _in_proj_kernel(x_ref, g_ref, wm_ref, ws_ref, wg_ref, cw_ref, cb_ref, main_ref, gate_ref,
                    h_ref, buf_ref, halo_ref):
    @pl.when(pl.program_id(1) == 0)
    def _():
        halo_ref[...] = jnp.zeros_like(halo_ref)

    h_slot = h_ref.at[pl.program_id(1) % 2]
    h_slot[...] = _rms(x_ref[0], g_ref[...]).astype(BF16)
    gate_ref[0] = _dot(h_slot[...], wg_ref[...])
    sb_q = 4 * MLSTM_WIDTH
    n_conv = 2 * MLSTM_WIDTH // PROJ_CHUNK
    n_all = MAIN_COLS // PROJ_CHUNK
    plain = list(range(n_conv, n_all))
    per = -(-len(plain) // n_conv)
    order = []
    for c in range(n_conv):
        order += [c] + plain[c * per:(c + 1) * per]
    for c in order:
        lo, hi = c * PROJ_CHUNK, (c + 1) * PROJ_CHUNK
        w = wm_ref[:, lo:hi] if hi <= sb_q else ws_ref[:, lo - sb_q:hi - sb_q]
        p = _dot(h_slot[...], w)
        if hi <= 2 * MLSTM_WIDTH:
            taps = [cw_ref[k:k + 1, lo:hi] for k in range(MLSTM_CONV)]
            y = _causal_taps(buf_ref, halo_ref.at[c], p, taps, MLSTM_CONV) + cb_ref[:, lo:hi]
            p = y * jax.nn.sigmoid(y)
            if hi <= MLSTM_WIDTH:
                p = p * MLSTM_HEAD_DIM ** -0.5
        elif sb_q <= lo and hi <= sb_q + SB_WIDTH:
            p = p * (SB_HEAD_DIM ** -0.5 * LOG2E)
        main_ref[0, :, lo:hi] = p.astype(BF16)


def _in_proj(x, g, w_mlstm, w_sb, w_gate, conv_w, conv_b):
    b, s, d = x.shape
    const = lambda *_: (0, 0)
    return pl.pallas_call(
        _in_proj_kernel,
        grid=(b, s // SEQ_TILE),
        in_specs=[
            pl.BlockSpec((1, SEQ_TILE, d), lambda i, j: (i, j, 0)),
            pl.BlockSpec((1, d), const),
            pl.BlockSpec((d, 4 * MLSTM_WIDTH), const),
            pl.BlockSpec((d, 3 * SB_WIDTH), const),
            pl.BlockSpec((d, GATE_COLS), const),
            pl.BlockSpec((MLSTM_CONV, 2 * MLSTM_WIDTH), const),
            pl.BlockSpec((1, 2 * MLSTM_WIDTH), const),
        ],
        out_specs=[
            pl.BlockSpec((1, SEQ_TILE, MAIN_COLS), lambda i, j: (i, j, 0)),
            pl.BlockSpec((1, SEQ_TILE, GATE_COLS), lambda i, j: (i, j, 0)),
        ],
        out_shape=[
            jax.ShapeDtypeStruct((b, s, MAIN_COLS), BF16),
            jax.ShapeDtypeStruct((b, s, GATE_COLS), F32),
        ],
        scratch_shapes=[
            pltpu.VMEM((2, SEQ_TILE, d), BF16),
            pltpu.VMEM((SEQ_TILE + SUBLANES, PROJ_CHUNK), F32),
            pltpu.VMEM((2 * MLSTM_WIDTH // PROJ_CHUNK, SUBLANES, PROJ_CHUNK), F32),
        ],
        compiler_params=pltpu.CompilerParams(
            dimension_semantics=("arbitrary", "arbitrary"), vmem_limit_bytes=VMEM_LIMIT),
        name="in_proj",
    )(x, g, w_mlstm, w_sb, w_gate, conv_w, conv_b)


def _mlstm_setup(q_ref, k_ref, v_ref, o_ref, gate_ref, gb_ref, nw_ref, out_ref,
                 state_ref, a_ref, dec_ref, col_ref):
    L = MLSTM_CHUNK
    D = MLSTM_HEAD_DIM
    H = MLSTM_HEADS
    R = SUBLANES
    n_chunks = q_ref.shape[1] // L
    assert L >= 2 * D and 4 * R <= LANES

    gt = (gate_ref[0] + gb_ref[...]).T[:R, :]
    lf = _log_sigmoid(pltpu.roll(gt, H, axis=0))
    by_chunk = lambda x: jnp.concatenate([x[:, c * L:(c + 1) * L] for c in range(n_chunks)], axis=0)
    u = lax.broadcasted_iota(jnp.int32, (L, L), 0)
    t = lax.broadcasted_iota(jnp.int32, (L, L), 1)
    b = jnp.dot(by_chunk(lf), (u <= t).astype(F32), preferred_element_type=F32,
                precision=lax.Precision.HIGHEST)
    a = by_chunk(gt) - b
    lane = lax.broadcasted_iota(jnp.int32, a.shape, 1)
    amax = a
    shift = 1
    while shift < L:
        amax = jnp.maximum(amax, jnp.where(lane >= shift, pltpu.roll(amax, shift, axis=1), NEG_BIG))
        shift *= 2
    last = lambda x: jnp.broadcast_to(x[:, L - 1:L], x.shape)
    b_last, a_last = last(b), last(amax)
    m = jnp.zeros((R, L), F32)
    mu, m_prev, m_next = [], [], []
    for c in range(n_chunks):
        rows = slice(c * R, (c + 1) * R)
        mu.append(jnp.maximum(amax[rows], m))
        m_prev.append(m)
        m = b_last[rows] + jnp.maximum(a_last[rows], m)
        m_next.append(m)
    mu, m_prev, m_next = (jnp.concatenate(x, axis=0) for x in (mu, m_prev, m_next))
    w_inter = jnp.exp(m_prev - mu)
    inv_floor = jnp.exp(-(b + mu))
    w_k = jnp.exp(b_last + a - m_next)
    dec = jnp.exp(b_last + m_prev - m_next)
    pad = jnp.zeros((LANES - 4 * R, L), F32)
    for c in range(n_chunks):
        rows = slice(c * R, (c + 1) * R)
        a_ref[c] = a[rows] * LOG2E
        dec_ref[c] = dec[rows]
        col_ref[c] = jnp.concatenate(
            [mu[rows] * LOG2E, w_inter[rows], inv_floor[rows], w_k[rows], pad], axis=0).T

    state_ref[...] = jnp.zeros_like(state_ref)
    causal = t <= u
    ones_blk = jnp.ones((L, D), BF16)
    mean_sq = jnp.full((D, D), 1.0 / D, BF16)

    def chunk(c, carry):
        t0 = c * L if isinstance(c, int) else pl.multiple_of(c * L, L)
        for h in range(H):
            hs = slice(h * D, (h + 1) * D)
            q = q_ref[0, pl.ds(t0, L), hs]
            k = k_ref[0, pl.ds(t0, L), hs]
            v = v_ref[0, pl.ds(t0, L), hs]
            a_row = a_ref[c, h:h + 1, :]
            decay = dec_ref[c, h:h + 1, :2 * D]
            mu_c, wi_c, fl_c, wk_c = (col_ref[c, :, j * R + h:j * R + h + 1] for j in range(4))
            state = state_ref[h]

            w = jnp.where(causal, jnp.exp2(a_row - mu_c), 0.0)
            scores = (_dot_nt(q, k) * w).astype(BF16)
            nd = (_dot(scores, jnp.concatenate([v, ones_blk], axis=1))
                  + wi_c * _dot(q, state.astype(BF16)))
            hc = nd[:, :D] / jnp.maximum(jnp.abs(nd[:, D:]), fl_c)

            wv = jnp.concatenate([(wk_c * v.astype(F32)).astype(BF16),
                                  jnp.broadcast_to(wk_c, (L, D)).astype(BF16)], axis=1)
            state_ref[h] = decay * state + _dot_tn(k, wv)

            ms = _dot((hc * hc).astype(BF16), mean_sq)
            hn = hc * lax.rsqrt(ms + EPS) * nw_ref[:, hs]
            gate = jax.nn.sigmoid(o_ref[0, pl.ds(t0, L), hs].astype(F32))
            out_ref[0, pl.ds(t0, L), hs] = (gate * hn).astype(BF16)
        return carry

    return chunk


def _sb_setup(q_ref, k_ref, v_ref, out_ref, acc_ref, carry_ref):
    T = SB_BLOCK
    pairs = SB_WIDTH // LANES
    lane = lax.broadcasted_iota(jnp.int32, (T, LANES), 1)
    head0 = lane < SB_HEAD_DIM
    row2 = lax.broadcasted_iota(jnp.int32, (2 * T, T), 0)
    col2 = lax.broadcasted_iota(jnp.int32, (2 * T, T), 1)
    strict = col2 < jnp.bitwise_and(row2, T - 1)
    row = lax.broadcasted_iota(jnp.int32, (T, T), 0)
    col = lax.broadcasted_iota(jnp.int32, (T, T), 1)
    later = (row > col).astype(BF16)

    def lanes(p):
        return slice(p * LANES, (p + 1) * LANES)

    def stacked_q(t0, p):
        q = q_ref[0, pl.ds(t0, T), lanes(p)]
        zero = jnp.zeros_like(q)
        return jnp.concatenate([jnp.where(head0, q, zero), jnp.where(head0, zero, q)], axis=0)

    def tile(qq, s0, p, shift, mask):
        k = k_ref[0, pl.ds(s0, T), lanes(p)]
        v = v_ref[0, pl.ds(s0, T), lanes(p)]
        z = _dot_nt(qq, k)
        neg_abs = lax.bitcast_convert_type(
            lax.bitcast_convert_type(z, jnp.uint32) | jnp.uint32(0x80000000), F32)
        t = jnp.minimum(z, 0.0) - jnp.log2(1.0 + jnp.exp2(neg_abs))
        sp = z - t
        if mask is not None:
            sp = jnp.where(mask, sp, 0.0)
        r = _dot(sp.astype(BF16), later)
        arg = t - r
        if shift is not None:
            arg = arg - shift
        a = jnp.exp2(arg)
        if mask is not None:
            a = jnp.where(mask, a, 0.0)
        o = _dot(a.astype(BF16), v)
        total = r[:, :1] + sp[:, :1]
        return o, total

    def merge_heads(o):
        return jnp.where(head0, o[:T], o[T:]).astype(BF16)

    def first():
        for p in range(pairs):
            o, _ = tile(stacked_q(0, p), 0, p, None, strict)
            out_ref[0, 0:T, lanes(p)] = merge_heads(o)

    def q_block(i, carry):
        t0 = pl.multiple_of(i * T, T)
        t1 = pl.multiple_of(t0 - T, T)
        low = None
        for p in range(pairs):
            qq = stacked_q(t0, p)
            o0, tot0 = tile(qq, t0, p, None, strict)
            o1, tot1 = tile(qq, t1, p, tot0, None)
            acc_ref[p] = o0 + o1
            cp = tot0 + tot1
            carry_ref[p] = cp
            m = jnp.min(cp)
            low = m if low is None else jnp.minimum(low, m)

        def more(state):
            d, low = state
            return jnp.logical_and(d <= i, low < SB_SKIP_LOG2)

        def k_block(state):
            d, _ = state
            s0 = pl.multiple_of((i - d) * T, T)
            low = None
            for p in range(pairs):
                cp = carry_ref[p]
                o, tot = tile(stacked_q(t0, p), s0, p, cp, None)
                acc_ref[p] += o
                cp = cp + tot
                carry_ref[p] = cp
                m = jnp.min(cp)
                low = m if low is None else jnp.minimum(low, m)
            return d + 1, low

        lax.while_loop(more, k_block, (jnp.int32(2), low))
        for p in range(pairs):
            out_ref[0, pl.ds(t0, T), lanes(p)] = merge_heads(acc_ref[p])
        return carry

    return first, q_block


def _mix_kernel(qm_ref, km_ref, vm_ref, om_ref, gate_ref, gb_ref, nw_ref, qs_ref, ks_ref, vs_ref,
                hm_ref, hs_ref, state_ref, a_ref, dec_ref, col_ref, acc_ref, carry_ref):
    assert MLSTM_CHUNK == SB_BLOCK
    chunk = _mlstm_setup(qm_ref, km_ref, vm_ref, om_ref, gate_ref, gb_ref, nw_ref, hm_ref,
                         state_ref, a_ref, dec_ref, col_ref)
    first, q_block = _sb_setup(qs_ref, ks_ref, vs_ref, hs_ref, acc_ref, carry_ref)
    chunk(0, 0)
    first()

    def body(i, carry):
        chunk(i, carry)
        return q_block(i, carry)

    lax.fori_loop(1, qm_ref.shape[1] // SB_BLOCK, body, 0)


def _mix(main, gates, gate_bias, norm_w):
    b, s, _ = main.shape
    assert MLSTM_WIDTH == SB_WIDTH
    w = MLSTM_WIDTH
    col = lambda c: pl.BlockSpec((1, s, w), lambda i, c=c: (i, 0, c))
    n = s // MLSTM_CHUNK
    return pl.pallas_call(
        _mix_kernel,
        grid=(b,),
        in_specs=[
            col(0), col(1), col(2), col(3),
            pl.BlockSpec((1, s, GATE_COLS), lambda i: (i, 0, 0)),
            pl.BlockSpec((1, GATE_COLS), lambda i: (0, 0)),
            pl.BlockSpec((1, w), lambda i: (0, 0)),
            col(4), col(5), col(6),
        ],
        out_specs=[pl.BlockSpec((1, s, w), lambda i: (i, 0, 0)),
                   pl.BlockSpec((1, s, w), lambda i: (i, 0, 0))],
        out_shape=[jax.ShapeDtypeStruct((b, s, w), BF16), jax.ShapeDtypeStruct((b, s, w), BF16)],
        scratch_shapes=[
            pltpu.VMEM((MLSTM_HEADS, MLSTM_HEAD_DIM, 2 * MLSTM_HEAD_DIM), F32),
            pltpu.VMEM((n, SUBLANES, MLSTM_CHUNK), F32),
            pltpu.VMEM((n, SUBLANES, MLSTM_CHUNK), F32),
            pltpu.VMEM((n, MLSTM_CHUNK, LANES), F32),
            pltpu.VMEM((SB_WIDTH // LANES, 2 * SB_BLOCK, LANES), F32),
            pltpu.VMEM((SB_WIDTH // LANES, 2 * SB_BLOCK, 1), F32),
        ],
        compiler_params=pltpu.CompilerParams(
            dimension_semantics=("arbitrary",), vmem_limit_bytes=VMEM_LIMIT),
        name="mix",
    )(main, main, main, main, gates, gate_bias, norm_w, main, main, main)


def _out_ffn_kernel(x_ref, hm_ref, hs_ref, wo_ref, g1_ref, g2_ref, wu_ref, cw_ref, cb_ref,
                    wd_ref, g3_ref, out_ref, x1_ref, h_ref, act_ref, buf_ref, halo_ref):
    @pl.when(pl.program_id(1) == 0)
    def _():
        halo_ref[...] = jnp.zeros_like(halo_ref)

    sub = SEQ_TILE // FFN_SPLIT
    for r in range(FFN_SPLIT):
        rows = slice(r * sub, (r + 1) * sub)
        mix = (_dot(hm_ref[0, rows, :], wo_ref[:MLSTM_WIDTH, :])
               + _dot(hs_ref[0, rows, :], wo_ref[MLSTM_WIDTH:, :]))
        x1 = x_ref[0, rows, :] + _rms(mix, g1_ref[...])
        x1_ref[rows, :] = x1
        h_ref[rows, :] = _rms(x1, g2_ref[...]).astype(BF16)

    for r in range(FFN_SPLIT):
        rows = slice(r * sub, (r + 1) * sub)
        for c in range(D_FF // FF_CHUNK):
            lo, hi = c * FF_CHUNK, (c + 1) * FF_CHUNK
            h = h_ref[rows, :]
            gate = _dot(h, wu_ref[:, lo:hi])
            up = _dot(h, wu_ref[:, D_FF + lo:D_FF + hi])
            taps = [cw_ref[k:k + 1, lo:hi] for k in range(FFN_CONV)]
            y = _causal_taps(buf_ref.at[r], halo_ref.at[c], gate, taps, FFN_CONV) + cb_ref[:, lo:hi]
            act_ref[rows, lo:hi] = (jax.nn.gelu(y, approximate=True) * up).astype(BF16)
        y = _dot(act_ref[rows, :], wd_ref[...])
        out_ref[0, rows, :] = x1_ref[rows, :] + _rms(y, g3_ref[...])


def _out_ffn(x, h_m, h_s, w_out, g1, g2, w_up, conv_w, conv_b, w_down, g3):
    b, s, d = x.shape
    const = lambda *_: (0, 0)
    tile = lambda w: pl.BlockSpec((1, SEQ_TILE, w), lambda i, j: (i, j, 0))
    resident = lambda shape: pl.BlockSpec(shape, const, pipeline_mode=pl.Buffered(1))
    return pl.pallas_call(
        _out_ffn_kernel,
        grid=(b, s // SEQ_TILE),
        in_specs=[
            tile(d), tile(MLSTM_WIDTH), tile(SB_WIDTH),
            resident((d, d)),
            pl.BlockSpec((1, d), const), pl.BlockSpec((1, d), const),
            resident((d, 2 * D_FF)),
            pl.BlockSpec((FFN_CONV, D_FF), const), pl.BlockSpec((1, D_FF), const),
            resident((D_FF, d)),
            pl.BlockSpec((1, d), const),
        ],
        out_specs=tile(d),
        out_shape=jax.ShapeDtypeStruct((b, s, d), F32),
        scratch_shapes=[
            pltpu.VMEM((SEQ_TILE, d), F32),
            pltpu.VMEM((SEQ_TILE, d), BF16),
            pltpu.VMEM((SEQ_TILE, D_FF), BF16),
            pltpu.VMEM((FFN_SPLIT, SEQ_TILE // FFN_SPLIT + SUBLANES, FF_CHUNK), F32),
            pltpu.VMEM((D_FF // FF_CHUNK, SUBLANES, FF_CHUNK), F32),
        ],
        compiler_params=pltpu.CompilerParams(
            dimension_semantics=("arbitrary", "arbitrary"), vmem_limit_bytes=VMEM_LIMIT),
        name="out_ffn",
    )(x, h_m, h_s, w_out, g1, g2, w_up, conv_w, conv_b, w_down, g3)


@jax.jit
def kernel(x, pre_mix_norm, w_in, mlstm_conv_w, mlstm_conv_b, mlstm_b_i, mlstm_b_f, mlstm_norm,
           w_out, post_mix_norm, pre_ffn_norm, w_up, ffn_conv_w, ffn_conv_b, w_down,
           post_ffn_norm):
    depth = w_in.shape[0]
    o_gate = 4 * MLSTM_WIDTH
    o_sb = o_gate + 2 * MLSTM_HEADS
    for l in range(depth):
        w = w_in[l]
        w_gate = jnp.pad(w[:, o_gate:o_sb].astype(BF16), ((0, 0), (0, GATE_COLS - 2 * MLSTM_HEADS)))
        gate_bias = jnp.pad(jnp.concatenate([mlstm_b_i[l], mlstm_b_f[l]]),
                            (0, GATE_COLS - 2 * MLSTM_HEADS))[None, :]
        main, gates = _in_proj(x, pre_mix_norm[l][None, :], w[:, :o_gate].astype(BF16),
                               w[:, o_sb:].astype(BF16), w_gate,
                               mlstm_conv_w[l], mlstm_conv_b[l][None, :])
        h_m, h_s = _mix(main, gates, gate_bias, mlstm_norm[l][None, :])
        x = _out_ffn(x, h_m, h_s, w_out[l].astype(BF16), post_mix_norm[l][None, :],
                     pre_ffn_norm[l][None, :], w_up[l].astype(BF16), ffn_conv_w[l],
                     ffn_conv_b[l][None, :], w_down[l].astype(BF16), post_ffn_norm[l][None, :])
    return x
```

```python
import jax
import jax.numpy as jnp
from jax import lax
from jax.experimental import pallas as pl
from jax.experimental.pallas import tpu as pltpu

D_MODEL = 1024
MLSTM_WIDTH = 512
MLSTM_HEADS = 4
MLSTM_HEAD_DIM = 128
MLSTM_CONV = 4
SB_WIDTH = 512
SB_HEADS = 8
SB_HEAD_DIM = 64
D_FF = 2816
FFN_CONV = 3
EPS = 1e-6

LANES = 128
SUBLANES = 8
BF16_ROWS = 16
MAIN_COLS = 4 * MLSTM_WIDTH + 3 * SB_WIDTH
GATE_COLS = LANES
PROJ_CHUNK = 256
SEQ_TILE = 1024
MLSTM_CHUNK = 256
SB_BLOCK = 256
LOG2E = 1.4426950408889634
SB_SKIP_LOG2 = 151.0
FF_CHUNK = 256
FFN_SPLIT = 4
VMEM_LIMIT = 56 * 1024 * 1024
NEG_BIG = -1e30

F32 = jnp.float32
BF16 = jnp.bfloat16


def _rms(v, w):
    return v * lax.rsqrt(jnp.mean(v * v, axis=-1, keepdims=True) + EPS) * w


def _log_sigmoid(v):
    return jnp.minimum(v, 0.0) - jnp.log1p(jnp.exp(-jnp.abs(v)))


def _dot(a, b):
    return jnp.dot(a, b, preferred_element_type=F32)


def _dot_nt(a, b):
    return lax.dot_general(a, b, (((1,), (1,)), ((), ())), preferred_element_type=F32)


def _dot_tn(a, b):
    return lax.dot_general(a, b, (((0,), (0,)), ((), ())), preferred_element_type=F32)


def _causal_taps(buf_ref, halo_ref, p, taps, width):
    n = p.shape[0]
    buf_ref[0:SUBLANES, :] = halo_ref[...]
    buf_ref[SUBLANES:SUBLANES + n, :] = p
    halo_ref[...] = p[n - SUBLANES:, :]
    y = taps[width - 1] * p
    for j in range(1, width):
        y = y + taps[width - 1 - j] * buf_ref[SUBLANES - j:SUBLANES - j + n, :]
    return y


def _in_proj_kernel(x_ref, g_ref, wm_ref, ws_ref, wg_ref, cw_ref, cb_ref, *rest):
    n_cast = (len(rest) - 5) // 2
    cast_in, (main_ref, gate_ref) = rest[:n_cast], rest[n_cast:n_cast + 2]
    cast_out, (h_ref, buf_ref, halo_ref) = rest[n_cast + 2:2 * n_cast + 2], rest[2 * n_cast + 2:]

    @pl.when(pl.program_id(1) == 0)
    def _():
        halo_ref[...] = jnp.zeros_like(halo_ref)

    for src, dst in zip(cast_in, cast_out):
        dst[...] = src[...].astype(BF16)

    h_slot = h_ref.at[pl.program_id(1) % 2]
    h_slot[...] = _rms(x_ref[0], g_ref[...]).astype(BF16)
    gate_ref[0] = _dot(h_slot[...], wg_ref[...])
    sb_q = 4 * MLSTM_WIDTH
    n_conv = 2 * MLSTM_WIDTH // PROJ_CHUNK
    n_all = MAIN_COLS // PROJ_CHUNK
    plain = list(range(n_conv, n_all))
    per = -(-len(plain) // n_conv)
    order = []
    for c in range(n_conv):
        order += [c] + plain[c * per:(c + 1) * per]
    for c in order:
        lo, hi = c * PROJ_CHUNK, (c + 1) * PROJ_CHUNK
        w = wm_ref[:, lo:hi] if hi <= sb_q else ws_ref[:, lo - sb_q:hi - sb_q]
        p = _dot(h_slot[...], w)
        if hi <= 2 * MLSTM_WIDTH:
            taps = [cw_ref[k:k + 1, lo:hi] for k in range(MLSTM_CONV)]
            y = _causal_taps(buf_ref, halo_ref.at[c], p, taps, MLSTM_CONV) + cb_ref[:, lo:hi]
            p = y * jax.nn.sigmoid(y)
            if hi <= MLSTM_WIDTH:
                p = p * MLSTM_HEAD_DIM ** -0.5
        elif sb_q <= lo and hi <= sb_q + SB_WIDTH:
            p = p * (SB_HEAD_DIM ** -0.5 * LOG2E)
        main_ref[0, :, lo:hi] = p.astype(BF16)


def _cast_spec(rows, cols, n_steps, n_inner):
    n_blocks = max(k for k in range(1, n_steps + 1)
                   if rows % k == 0 and (rows // k) % BF16_ROWS == 0)
    return pl.BlockSpec((rows // n_blocks, cols),
                        lambda i, j: (jnp.minimum(i * n_inner + j, n_blocks - 1), 0))


def _in_proj(x, g, w_mlstm, w_sb, w_gate, conv_w, conv_b, to_cast):
    b, s, d = x.shape
    const = lambda *_: (0, 0)
    n_inner = s // SEQ_TILE
    cast_specs = [_cast_spec(*w.shape, b * n_inner, n_inner) for w in to_cast]
    return pl.pallas_call(
        _in_proj_kernel,
        grid=(b, n_inner),
        in_specs=[
            pl.BlockSpec((1, SEQ_TILE, d), lambda i, j: (i, j, 0)),
            pl.BlockSpec((1, d), const),
            pl.BlockSpec((d, 4 * MLSTM_WIDTH), const),
            pl.BlockSpec((d, 3 * SB_WIDTH), const),
            pl.BlockSpec((d, GATE_COLS), const),
            pl.BlockSpec((MLSTM_CONV, 2 * MLSTM_WIDTH), const),
            pl.BlockSpec((1, 2 * MLSTM_WIDTH), const),
        ] + cast_specs,
        out_specs=[
            pl.BlockSpec((1, SEQ_TILE, MAIN_COLS), lambda i, j: (i, j, 0)),
            pl.BlockSpec((1, SEQ_TILE, GATE_COLS), lambda i, j: (i, j, 0)),
        ] + cast_specs,
        out_shape=[
            jax.ShapeDtypeStruct((b, s, MAIN_COLS), BF16),
            jax.ShapeDtypeStruct((b, s, GATE_COLS), F32),
        ] + [jax.ShapeDtypeStruct(w.shape, BF16) for w in to_cast],
        scratch_shapes=[
            pltpu.VMEM((2, SEQ_TILE, d), BF16),
            pltpu.VMEM((SEQ_TILE + SUBLANES, PROJ_CHUNK), F32),
            pltpu.VMEM((2 * MLSTM_WIDTH // PROJ_CHUNK, SUBLANES, PROJ_CHUNK), F32),
        ],
        compiler_params=pltpu.CompilerParams(
            dimension_semantics=("arbitrary", "arbitrary"), vmem_limit_bytes=VMEM_LIMIT),
        name="in_proj",
    )(x, g, w_mlstm, w_sb, w_gate, conv_w, conv_b, *to_cast)


def _mlstm_setup(q_ref, k_ref, v_ref, o_ref, gate_ref, gb_ref, nw_ref, out_ref,
                 state_ref, a_ref, dec_ref, col_ref):
    L = MLSTM_CHUNK
    D = MLSTM_HEAD_DIM
    H = MLSTM_HEADS
    R = SUBLANES
    n_chunks = q_ref.shape[1] // L
    assert L >= 2 * D and 4 * R <= LANES

    gt = (gate_ref[0] + gb_ref[...]).T[:R, :]
    lf = _log_sigmoid(pltpu.roll(gt, H, axis=0))
    by_chunk = lambda x: jnp.concatenate([x[:, c * L:(c + 1) * L] for c in range(n_chunks)], axis=0)
    u = lax.broadcasted_iota(jnp.int32, (L, L), 0)
    t = lax.broadcasted_iota(jnp.int32, (L, L), 1)
    b = jnp.dot(by_chunk(lf), (u <= t).astype(F32), preferred_element_type=F32,
                precision=lax.Precision.HIGHEST)
    a = by_chunk(gt) - b
    lane = lax.broadcasted_iota(jnp.int32, a.shape, 1)
    amax = a
    shift = 1
    while shift < L:
        amax = jnp.maximum(amax, jnp.where(lane >= shift, pltpu.roll(amax, shift, axis=1), NEG_BIG))
        shift *= 2
    last = lambda x: jnp.broadcast_to(x[:, L - 1:L], x.shape)
    b_last, a_last = last(b), last(amax)
    m = jnp.zeros((R, L), F32)
    mu, m_prev, m_next = [], [], []
    for c in range(n_chunks):
        rows = slice(c * R, (c + 1) * R)
        mu.append(jnp.maximum(amax[rows], m))
        m_prev.append(m)
        m = b_last[rows] + jnp.maximum(a_last[rows], m)
        m_next.append(m)
    mu, m_prev, m_next = (jnp.concatenate(x, axis=0) for x in (mu, m_prev, m_next))
    w_inter = jnp.exp(m_prev - mu)
    inv_floor = jnp.exp(-(b + mu))
    w_k = jnp.exp(b_last + a - m_next)
    dec = jnp.exp(b_last + m_prev - m_next)
    pad = jnp.zeros((LANES - 4 * R, L), F32)
    for c in range(n_chunks):
        rows = slice(c * R, (c + 1) * R)
        a_ref[c] = a[rows] * LOG2E
        dec_ref[c] = dec[rows]
        col_ref[c] = jnp.concatenate(
            [mu[rows] * LOG2E, w_inter[rows], inv_floor[rows], w_k[rows], pad], axis=0).T

    state_ref[...] = jnp.zeros_like(state_ref)
    causal = t <= u
    ones_blk = jnp.ones((L, D), BF16)
    mean_sq = jnp.full((D, D), 1.0 / D, BF16)

    def chunk(c, carry):
        t0 = c * L if isinstance(c, int) else pl.multiple_of(c * L, L)
        for h in range(H):
            hs = slice(h * D, (h + 1) * D)
            q = q_ref[0, pl.ds(t0, L), hs]
            k = k_ref[0, pl.ds(t0, L), hs]
            v = v_ref[0, pl.ds(t0, L), hs]
            a_row = a_ref[c, h:h + 1, :]
            decay = dec_ref[c, h:h + 1, :2 * D]
            mu_c, wi_c, fl_c, wk_c = (col_ref[c, :, j * R + h:j * R + h + 1] for j in range(4))
            state = state_ref[h]

            w = jnp.where(causal, jnp.exp2(a_row - mu_c), 0.0)
            scores = (_dot_nt(q, k) * w).astype(BF16)
            nd = (_dot(scores, jnp.concatenate([v, ones_blk], axis=1))
                  + wi_c * _dot(q, state.astype(BF16)))
            hc = nd[:, :D] / jnp.maximum(jnp.abs(nd[:, D:]), fl_c)

            wv = jnp.concatenate([(wk_c * v.astype(F32)).astype(BF16),
                                  jnp.broadcast_to(wk_c, (L, D)).astype(BF16)], axis=1)
            state_ref[h] = decay * state + _dot_tn(k, wv)

            ms = _dot((hc * hc).astype(BF16), mean_sq)
            hn = hc * lax.rsqrt(ms + EPS) * nw_ref[:, hs]
            gate = jax.nn.sigmoid(o_ref[0, pl.ds(t0, L), hs].astype(F32))
            out_ref[0, pl.ds(t0, L), hs] = (gate * hn).astype(BF16)
        return carry

    return chunk


def _sb_setup(q_ref, k_ref, v_ref, out_ref, acc_ref, carry_ref):
    T = SB_BLOCK
    pairs = SB_WIDTH // LANES
    lane = lax.broadcasted_iota(jnp.int32, (T, LANES), 1)
    head0 = lane < SB_HEAD_DIM
    row2 = lax.broadcasted_iota(jnp.int32, (2 * T, T), 0)
    col2 = lax.broadcasted_iota(jnp.int32, (2 * T, T), 1)
    strict = col2 < jnp.bitwise_and(row2, T - 1)
    row = lax.broadcasted_iota(jnp.int32, (T, T), 0)
    col = lax.broadcasted_iota(jnp.int32, (T, T), 1)
    later = (row > col).astype(BF16)

    def lanes(p):
        return slice(p * LANES, (p + 1) * LANES)

    def stacked_q(t0, p):
        q = q_ref[0, pl.ds(t0, T), lanes(p)]
        zero = jnp.zeros_like(q)
        return jnp.concatenate([jnp.where(head0, q, zero), jnp.where(head0, zero, q)], axis=0)

    def tile(qq, s0, p, shift, mask):
        k = k_ref[0, pl.ds(s0, T), lanes(p)]
        v = v_ref[0, pl.ds(s0, T), lanes(p)]
        z = _dot_nt(qq, k)
        neg_abs = -jnp.abs(z)
        t = jnp.minimum(z, 0.0) - jnp.log2(1.0 + jnp.exp2(neg_abs))
        sp = z - t
        if mask is not None:
            sp = jnp.where(mask, sp, 0.0)
        r = _dot(sp.astype(BF16), later)
        arg = t - r
        if shift is not None:
            arg = arg - shift
        a = jnp.exp2(arg)
        if mask is not None:
            a = jnp.where(mask, a, 0.0)
        o = _dot(a.astype(BF16), v)
        total = r[:, :1] + sp[:, :1]
        return o, total

    def merge_heads(o):
        return jnp.where(head0, o[:T], o[T:]).astype(BF16)

    def first():
        for p in range(pairs):
            o, _ = tile(stacked_q(0, p), 0, p, None, strict)
            out_ref[0, 0:T, lanes(p)] = merge_heads(o)

    def q_block(i, carry):
        t0 = pl.multiple_of(i * T, T)
        t1 = pl.multiple_of(t0 - T, T)
        low = None
        for p in range(pairs):
            qq = stacked_q(t0, p)
            o0, tot0 = tile(qq, t0, p, None, strict)
            o1, tot1 = tile(qq, t1, p, tot0, None)
            acc_ref[p] = o0 + o1
            cp = tot0 + tot1
            carry_ref[p] = cp
            m = jnp.min(cp)
            low = m if low is None else jnp.minimum(low, m)

        def more(state):
            d, low = state
            return jnp.logical_and(d <= i, low < SB_SKIP_LOG2)

        def k_block(state):
            d, _ = state
            s0 = pl.multiple_of((i - d) * T, T)
            low = None
            for p in range(pairs):
                cp = carry_ref[p]
                o, tot = tile(stacked_q(t0, p), s0, p, cp, None)
                acc_ref[p] += o
                cp = cp + tot
                carry_ref[p] = cp
                m = jnp.min(cp)
                low = m if low is None else jnp.minimum(low, m)
            return d + 1, low

        lax.while_loop(more, k_block, (jnp.int32(2), low))
        for p in range(pairs):
            out_ref[0, pl.ds(t0, T), lanes(p)] = merge_heads(acc_ref[p])
        return carry

    return first, q_block


def _mix_kernel(qm_ref, km_ref, vm_ref, om_ref, gate_ref, gb_ref, nw_ref, qs_ref, ks_ref, vs_ref,
                hm_ref, hs_ref, state_ref, a_ref, dec_ref, col_ref, acc_ref, carry_ref):
    assert MLSTM_CHUNK == SB_BLOCK
    chunk = _mlstm_setup(qm_ref, km_ref, vm_ref, om_ref, gate_ref, gb_ref, nw_ref, hm_ref,
                         state_ref, a_ref, dec_ref, col_ref)
    first, q_block = _sb_setup(qs_ref, ks_ref, vs_ref, hs_ref, acc_ref, carry_ref)
    chunk(0, 0)
    first()

    def body(i, carry):
        chunk(i, carry)
        return q_block(i, carry)

    lax.fori_loop(1, qm_ref.shape[1] // SB_BLOCK, body, 0)


def _mix(main, gates, gate_bias, norm_w):
    b, s, _ = main.shape
    assert MLSTM_WIDTH == SB_WIDTH
    w = MLSTM_WIDTH
    col = lambda c: pl.BlockSpec((1, s, w), lambda i, c=c: (i, 0, c))
    n = s // MLSTM_CHUNK
    return pl.pallas_call(
        _mix_kernel,
        grid=(b,),
        in_specs=[
            col(0), col(1), col(2), col(3),
            pl.BlockSpec((1, s, GATE_COLS), lambda i: (i, 0, 0)),
            pl.BlockSpec((1, GATE_COLS), lambda i: (0, 0)),
            pl.BlockSpec((1, w), lambda i: (0, 0)),
            col(4), col(5), col(6),
        ],
        out_specs=[pl.BlockSpec((1, s, w), lambda i: (i, 0, 0)),
                   pl.BlockSpec((1, s, w), lambda i: (i, 0, 0))],
        out_shape=[jax.ShapeDtypeStruct((b, s, w), BF16), jax.ShapeDtypeStruct((b, s, w), BF16)],
        scratch_shapes=[
            pltpu.VMEM((MLSTM_HEADS, MLSTM_HEAD_DIM, 2 * MLSTM_HEAD_DIM), F32),
            pltpu.VMEM((n, SUBLANES, MLSTM_CHUNK), F32),
            pltpu.VMEM((n, SUBLANES, MLSTM_CHUNK), F32),
            pltpu.VMEM((n, MLSTM_CHUNK, LANES), F32),
            pltpu.VMEM((SB_WIDTH // LANES, 2 * SB_BLOCK, LANES), F32),
            pltpu.VMEM((SB_WIDTH // LANES, 2 * SB_BLOCK, 1), F32),
        ],
        compiler_params=pltpu.CompilerParams(
            dimension_semantics=("arbitrary",), vmem_limit_bytes=VMEM_LIMIT),
        name="mix",
    )(main, main, main, main, gates, gate_bias, norm_w, main, main, main)


def _out_ffn_kernel(x_ref, hm_ref, hs_ref, wo_ref, g1_ref, g2_ref, wu_ref, cw_ref, cb_ref,
                    wd_ref, g3_ref, out_ref, x1_ref, h_ref, act_ref, buf_ref, halo_ref):
    @pl.when(pl.program_id(1) == 0)
    def _():
        halo_ref[...] = jnp.zeros_like(halo_ref)

    sub = SEQ_TILE // FFN_SPLIT
    for r in range(FFN_SPLIT):
        rows = slice(r * sub, (r + 1) * sub)
        mix = (_dot(hm_ref[0, rows, :], wo_ref[:MLSTM_WIDTH, :])
               + _dot(hs_ref[0, rows, :], wo_ref[MLSTM_WIDTH:, :]))
        x1 = x_ref[0, rows, :] + _rms(mix, g1_ref[...])
        x1_ref[rows, :] = x1
        h_ref[rows, :] = _rms(x1, g2_ref[...]).astype(BF16)

    for r in range(FFN_SPLIT):
        rows = slice(r * sub, (r + 1) * sub)
        for c in range(D_FF // FF_CHUNK):
            lo, hi = c * FF_CHUNK, (c + 1) * FF_CHUNK
            h = h_ref[rows, :]
            gate = _dot(h, wu_ref[:, lo:hi])
            up = _dot(h, wu_ref[:, D_FF + lo:D_FF + hi])
            taps = [cw_ref[k:k + 1, lo:hi] for k in range(FFN_CONV)]
            y = _causal_taps(buf_ref.at[r], halo_ref.at[c], gate, taps, FFN_CONV) + cb_ref[:, lo:hi]
            act_ref[rows, lo:hi] = (jax.nn.gelu(y, approximate=True) * up).astype(BF16)
        y = _dot(act_ref[rows, :], wd_ref[...])
        out_ref[0, rows, :] = x1_ref[rows, :] + _rms(y, g3_ref[...])


def _out_ffn(x, h_m, h_s, w_out, g1, g2, w_up, conv_w, conv_b, w_down, g3):
    b, s, d = x.shape
    const = lambda *_: (0, 0)
    tile = lambda w: pl.BlockSpec((1, SEQ_TILE, w), lambda i, j: (i, j, 0))
    resident = lambda shape: pl.BlockSpec(shape, const, pipeline_mode=pl.Buffered(1))
    return pl.pallas_call(
        _out_ffn_kernel,
        grid=(b, s // SEQ_TILE),
        in_specs=[
            tile(d), tile(MLSTM_WIDTH), tile(SB_WIDTH),
            resident((d, d)),
            pl.BlockSpec((1, d), const), pl.BlockSpec((1, d), const),
            resident((d, 2 * D_FF)),
            pl.BlockSpec((FFN_CONV, D_FF), const), pl.BlockSpec((1, D_FF), const),
            resident((D_FF, d)),
            pl.BlockSpec((1, d), const),
        ],
        out_specs=tile(d),
        out_shape=jax.ShapeDtypeStruct((b, s, d), F32),
        scratch_shapes=[
            pltpu.VMEM((SEQ_TILE, d), F32),
            pltpu.VMEM((SEQ_TILE, d), BF16),
            pltpu.VMEM((SEQ_TILE, D_FF), BF16),
            pltpu.VMEM((FFN_SPLIT, SEQ_TILE // FFN_SPLIT + SUBLANES, FF_CHUNK), F32),
            pltpu.VMEM((D_FF // FF_CHUNK, SUBLANES, FF_CHUNK), F32),
        ],
        compiler_params=pltpu.CompilerParams(
            dimension_semantics=("arbitrary", "arbitrary"), vmem_limit_bytes=VMEM_LIMIT),
        name="out_ffn",
    )(x, h_m, h_s, w_out, g1, g2, w_up, conv_w, conv_b, w_down, g3)


@jax.jit
def kernel(x, pre_mix_norm, w_in, mlstm_conv_w, mlstm_conv_b, mlstm_b_i, mlstm_b_f, mlstm_norm,
           w_out, post_mix_norm, pre_ffn_norm, w_up, ffn_conv_w, ffn_conv_b, w_down,
           post_ffn_norm):
    depth = w_in.shape[0]
    o_gate = 4 * MLSTM_WIDTH
    o_sb = o_gate + 2 * MLSTM_HEADS
    for l in range(depth):
        w = w_in[l]
        w_gate = jnp.pad(w[:, o_gate:o_sb].astype(BF16), ((0, 0), (0, GATE_COLS - 2 * MLSTM_HEADS)))
        gate_bias = jnp.pad(jnp.concatenate([mlstm_b_i[l], mlstm_b_f[l]]),
                            (0, GATE_COLS - 2 * MLSTM_HEADS))[None, :]
        main, gates, wo, wu, wd = _in_proj(
            x, pre_mix_norm[l][None, :], w[:, :o_gate].astype(BF16), w[:, o_sb:].astype(BF16), w_gate,
            mlstm_conv_w[l], mlstm_conv_b[l][None, :], [w_out[l], w_up[l], w_down[l]])
        h_m, h_s = _mix(main, gates, gate_bias, mlstm_norm[l][None, :])
        x = _out_ffn(x, h_m, h_s, wo, post_mix_norm[l][None, :], pre_ffn_norm[l][None, :], wu,
                     ffn_conv_w[l], ffn_conv_b[l][None, :], wd, post_ffn_norm[l][None, :])
    return x
```
